```python
import math
import functools
import jax
import jax.numpy as jnp
from jax import lax
import numpy as np

D_MODEL = 1024
BATCH = 2
SEQ = 8192
DEPTH = 2
DEC_BATCH = 32
DEC_SEQ = 8
PAST_LEN = 8192
PAGE_SIZE = 128

N_HEADS_A = 8
HD_A = D_MODEL // (2 * N_HEADS_A)
DV_A = 2 * HD_A
ATTN_W = N_HEADS_A * DV_A
ATTN_SCALE = HD_A ** -0.5
Q_BLOCK = 128
N_BUCKETS = 32
MAX_EXACT = N_BUCKETS // 2
MAX_DISTANCE = 128
EXPAND = 2
D_INNER = EXPAND * D_MODEL
HEADDIM_B = 64
N_HEADS_B = D_INNER // HEADDIM_B
N_GROUPS_B = 4
D_STATE = 128
CONV_W = 4
CONV_DIM = D_INNER + 2 * N_GROUPS_B * D_STATE
SSD_CHUNK = 128
D_FF = -(-8 * D_MODEL // (3 * 256)) * 256
EPS = 1e-6

Q_COLS = N_HEADS_A * 2 * HD_A
K_COLS = N_HEADS_A * 2 * HD_A
V_COLS = ATTN_W
Z_COLS = D_INNER
XBC_COLS = CONV_DIM
DT_COLS = N_HEADS_B
GATE_COLS = 2 * D_MODEL
IN_COLS = Q_COLS + K_COLS + V_COLS + Z_COLS + XBC_COLS + DT_COLS + GATE_COLS
SPLIT_POINTS = (
    Q_COLS,
    Q_COLS + K_COLS,
    Q_COLS + K_COLS + V_COLS,
    Q_COLS + K_COLS + V_COLS + Z_COLS,
    Q_COLS + K_COLS + V_COLS + Z_COLS + XBC_COLS,
    Q_COLS + K_COLS + V_COLS + Z_COLS + XBC_COLS + DT_COLS,
)

kernel_name = 'hybrid_diffattn_ssd_decode_step'


def rms_norm(x, w):
    x32 = x.astype(jnp.float32)
    y = x32 * lax.rsqrt(jnp.mean(x32 * x32, axis=-1, keepdims=True) + EPS)
    return (y * w.astype(jnp.float32)).astype(x.dtype)


def t5_bucket(dist):
    dist = jnp.maximum(dist, 0)
    d = jnp.maximum(dist, MAX_EXACT).astype(jnp.float32)
    large = MAX_EXACT + (jnp.log(d / MAX_EXACT) / math.log(MAX_DISTANCE / MAX_EXACT)
                         * (N_BUCKETS - MAX_EXACT)).astype(jnp.int32)
    large = jnp.minimum(large, N_BUCKETS - 1)
    return jnp.where(dist < MAX_EXACT, dist, large)


def diff_lambda(lq1, lk1, lq2, lk2, lam_init):
    f32 = jnp.float32
    return (jnp.exp(jnp.sum(lq1.astype(f32) * lk1.astype(f32)))
            - jnp.exp(jnp.sum(lq2.astype(f32) * lk2.astype(f32))) + lam_init)


def diff_logits(q, k, q_pos, k_pos, rel_bias):
    s = jnp.einsum('bqhmd,bkhmd->bhmqk', q, k).astype(jnp.float32) * ATTN_SCALE
    dist = q_pos[:, None] - k_pos[None, :]
    bias = jnp.transpose(rel_bias[t5_bucket(dist)].astype(jnp.float32), (2, 0, 1))
    s = s + bias[None, :, None]
    return jnp.where((dist >= 0)[None, None, None], s, -jnp.inf)


def diff_weights(logits, lam):
    p = jax.nn.softmax(logits, axis=-1)
    return p[:, :, 0] - lam * p[:, :, 1]


def attend_prompt(q, k, v, lam, rel_bias):
    b, s_len = q.shape[0], q.shape[1]
    k_pos = jnp.arange(s_len)

    def block(i):
        start = i * Q_BLOCK
        qb = lax.dynamic_slice_in_dim(q, start, Q_BLOCK, axis=1)
        q_pos = start + jnp.arange(Q_BLOCK)
        w = diff_weights(diff_logits(qb, k, q_pos, k_pos, rel_bias), lam).astype(v.dtype)
        return jnp.einsum('bhqk,bkhv->bqhv', w, v)

    out = lax.map(block, jnp.arange(s_len // Q_BLOCK))
    return jnp.moveaxis(out, 0, 1).reshape(b, s_len, N_HEADS_A, DV_A)


def attend_sample(q, k, v, k_past, v_past, lam, rel_bias):
    p_len, n_new = k_past.shape[1], q.shape[1]
    q_pos = p_len + jnp.arange(n_new)
    logits = jnp.concatenate([
        diff_logits(q, k_past, q_pos, jnp.arange(p_len), rel_bias),
        diff_logits(q, k, q_pos, q_pos, rel_bias)], axis=-1)
    w = diff_weights(logits, lam).astype(v.dtype)
    return (jnp.einsum('bhqk,bkhv->bqhv', w[..., :p_len], v_past)
            + jnp.einsum('bhqk,bkhv->bqhv', w[..., p_len:], v))


def diff_head_out(o, subln, lam_init):
    b, n = o.shape[0], o.shape[1]
    o = rms_norm(o, subln) * (1.0 - lam_init)
    return o.reshape(b, n, ATTN_W)


def causal_conv_silu(xbc, conv_prev, conv_w, conv_b):
    n = xbc.shape[1]
    xp = jnp.concatenate([conv_prev.astype(xbc.dtype), xbc], axis=1)
    y = conv_b + xp[:, 0:n] * conv_w[0]
    for j in range(1, CONV_W):
        y = y + xp[:, j:j + n] * conv_w[j]
    return jax.nn.silu(y), xp[:, n:]


def ssd(x, dt, a, bm, cm, d_skip, h0):
    f32 = jnp.float32
    bsz, n, n_h, p_dim = x.shape
    g_n, s_n = bm.shape[2], bm.shape[3]
    r_n = n_h // g_n
    cl = math.gcd(n, SSD_CHUNK)
    nc = n // cl
    xc = x.astype(f32).reshape(bsz, nc, cl, g_n, r_n, p_dim)
    dtc = dt.reshape(bsz, nc, cl, g_n, r_n)
    bc = bm.astype(f32).reshape(bsz, nc, cl, g_n, s_n)
    cc = cm.astype(f32).reshape(bsz, nc, cl, g_n, s_n)
    a_cs = jnp.cumsum(dtc * a.reshape(g_n, r_n), axis=2)
    xdt = xc * dtc[..., None]
    causal = jnp.tril(jnp.ones((cl, cl), dtype=bool))
    seg = a_cs[:, :, :, None] - a_cs[:, :, None, :]
    decay = jnp.exp(jnp.where(causal[:, :, None, None], seg, -jnp.inf))
    cb = jnp.einsum('bclgn,bcsgn->bclsg', cc, bc)
    y_diag = jnp.einsum('bclsgr,bcsgrp->bclgrp', cb[..., None] * decay, xdt)
    to_end = jnp.exp(a_cs[:, :, -1:] - a_cs)
    chunk_states = jnp.einsum('bclgn,bclgrp->bcgrpn', bc, xdt * to_end[..., None])
    chunk_decay = jnp.exp(a_cs[:, :, -1])

    def step(h, inp):
        s, dec = inp
        return dec[..., None, None] * h + s, h

    h_last, h_prev = lax.scan(
        step, h0.astype(f32).reshape(bsz, g_n, r_n, p_dim, s_n),
        (jnp.moveaxis(chunk_states, 1, 0), jnp.moveaxis(chunk_decay, 1, 0)))
    h_prev = jnp.moveaxis(h_prev, 0, 1)
    y_off = jnp.einsum('bclgn,bcgrpn->bclgrp', cc, h_prev) * jnp.exp(a_cs)[..., None]
    y = y_diag + y_off + d_skip.astype(f32).reshape(g_n, r_n)[:, :, None] * xc
    return y.reshape(bsz, n, n_h * p_dim), h_last.reshape(bsz, n_h, p_dim, s_n)


def gated_rms_norm(y, z, w):
    b, n = y.shape[0], y.shape[1]
    g = (y * jax.nn.silu(z.astype(jnp.float32))).reshape(b, n, N_GROUPS_B, D_INNER // N_GROUPS_B)
    g = g * lax.rsqrt(jnp.mean(g * g, axis=-1, keepdims=True) + EPS)
    return (g.reshape(b, n, D_INNER) * w.astype(jnp.float32)).astype(z.dtype)


def mamba_branch(z, xbc, dt_raw, conv_prev, ssm_prev, conv_w, conv_b, dt_bias, a_log, d_skip, ssm_norm):
    b, n = xbc.shape[0], xbc.shape[1]
    xbc, conv_new = causal_conv_silu(xbc, conv_prev, conv_w, conv_b)
    xs, bm, cm = jnp.split(xbc, (D_INNER, D_INNER + N_GROUPS_B * D_STATE), axis=-1)
    dt = jax.nn.softplus(dt_raw.astype(jnp.float32) + dt_bias.astype(jnp.float32))
    a = -jnp.exp(a_log.astype(jnp.float32))
    y, h = ssd(xs.reshape(b, n, N_HEADS_B, HEADDIM_B), dt, a,
               bm.reshape(b, n, N_GROUPS_B, D_STATE), cm.reshape(b, n, N_GROUPS_B, D_STATE),
               d_skip, ssm_prev)
    return gated_rms_norm(y, z, ssm_norm), conv_new, h


def merge_branches(attn_o, ssm_o, gates, w_br_a, w_br_b, w_out):
    g_a, g_b = jnp.split(gates, 2, axis=-1)
    ya = jnp.einsum('ble,ed->bld', attn_o, w_br_a)
    yb = jnp.einsum('ble,ed->bld', ssm_o, w_br_b)
    merged = jax.nn.sigmoid(g_a) * ya + jax.nn.sigmoid(g_b) * yb
    return jnp.einsum('bld,de->ble', merged, w_out)


def swiglu(u, wg, wu, wd):
    hid = jax.nn.silu(jnp.einsum('bld,df->blf', u, wg)) * jnp.einsum('bld,df->blf', u, wu)
    return jnp.einsum('blf,fd->bld', hid, wd)


def decoder_layer(h, attend, conv_prev, ssm_prev, lam_init, norm_mix_pre, w_in, attn_subln,
                  conv_w, conv_b, dt_bias, a_log, d_skip, ssm_norm, w_br_a, w_br_b, w_out,
                  norm_mix_post, norm_ffn_pre, w_ffn_gate, w_ffn_up, w_ffn_down, norm_ffn_post):
    b, n = h.shape[0], h.shape[1]
    u = rms_norm(h, norm_mix_pre)
    proj = jnp.einsum('bld,de->ble', u, w_in)
    q, k, v, z, xbc, dt_raw, gates = jnp.split(proj, SPLIT_POINTS, axis=-1)
    q = q.reshape(b, n, N_HEADS_A, 2, HD_A)
    k = k.reshape(b, n, N_HEADS_A, 2, HD_A)
    v = v.reshape(b, n, N_HEADS_A, DV_A)
    attn_o = diff_head_out(attend(q, k, v), attn_subln, lam_init)
    ssm_o, conv_new, ssm_new = mamba_branch(z, xbc, dt_raw, conv_prev, ssm_prev, conv_w, conv_b,
                                            dt_bias, a_log, d_skip, ssm_norm)
    mix = merge_branches(attn_o, ssm_o, gates, w_br_a, w_br_b, w_out)
    h = h + rms_norm(mix, norm_mix_post)
    ff = swiglu(rms_norm(h, norm_ffn_pre), w_ffn_gate, w_ffn_up, w_ffn_down)
    h = h + rms_norm(ff, norm_ffn_post)
    return h, k.reshape(b, n, N_HEADS_A, 2 * HD_A), v, conv_new, ssm_new.astype(h.dtype)


def setup_inputs(seed: int = 0) -> dict:
    key = jax.random.key(seed)
    ks = jax.random.split(key, 32)
    f32 = jnp.float32
    n_pages = PAST_LEN // PAGE_SIZE
    n_used = DEC_BATCH * n_pages
    n_pool = n_used + max(1, n_used // 4)

    def nrm(k, shape, scale):
        return scale * jax.random.normal(k, shape, f32)

    def gain(k, shape):
        return 1.0 + nrm(k, shape, 0.05)

    page_table = jax.random.permutation(ks[0], n_pool)[:n_used].reshape(DEC_BATCH, n_pages).astype(jnp.int32)
    dt0 = jnp.exp(jax.random.uniform(ks[1], (DEPTH, N_HEADS_B), f32, math.log(1e-3), math.log(1e-1)))
    dt_bias = dt0 + jnp.log(-jnp.expm1(-dt0))
    a_log = jnp.log(jax.random.uniform(ks[2], (DEPTH, N_HEADS_B), f32, 1.0, 16.0))
    return {
        'x_prompt': nrm(ks[3], (BATCH, SEQ, D_MODEL), 1.0),
        'x_sample': nrm(ks[4], (DEC_BATCH, DEC_SEQ, D_MODEL), 1.0),
        'cache_k': nrm(ks[5], (DEPTH, n_pool, PAGE_SIZE, N_HEADS_A, 2 * HD_A), 1.0),
        'cache_v': nrm(ks[6], (DEPTH, n_pool, PAGE_SIZE, N_HEADS_A, DV_A), 1.0),
        'state_ssm': nrm(ks[7], (DEPTH, DEC_BATCH, N_HEADS_B, HEADDIM_B, D_STATE), 0.1),
        'state_conv': nrm(ks[8], (DEPTH, DEC_BATCH, CONV_W - 1, CONV_DIM), 1.0),
        'page_table': page_table,
        'rel_bias': nrm(ks[9], (N_BUCKETS, N_HEADS_A), 0.3),
        'norm_mix_pre': gain(ks[10], (DEPTH, D_MODEL)),
        'w_in': nrm(ks[11], (DEPTH, D_MODEL, IN_COLS), D_MODEL ** -0.5),
        'lambda_q1': nrm(ks[12], (DEPTH, HD_A), 0.1),
        'lambda_k1': nrm(ks[13], (DEPTH, HD_A), 0.1),
        'lambda_q2': nrm(ks[14], (DEPTH, HD_A), 0.1),
        'lambda_k2': nrm(ks[15], (DEPTH, HD_A), 0.1),
        'attn_subln': gain(ks[16], (DEPTH, DV_A)),
        'conv_w': nrm(ks[17], (DEPTH, CONV_W, CONV_DIM), CONV_W ** -0.5),
        'conv_b': nrm(ks[18], (DEPTH, CONV_DIM), 0.02),
        'dt_bias': dt_bias,
        'a_log': a_log,
        'd_skip': gain(ks[19], (DEPTH, N_HEADS_B)),
        'ssm_norm': gain(ks[20], (DEPTH, D_INNER)),
        'w_br_a': nrm(ks[21], (DEPTH, ATTN_W, D_MODEL), ATTN_W ** -0.5),
        'w_br_b': nrm(ks[22], (DEPTH, D_INNER, D_MODEL), D_INNER ** -0.5),
        'w_out': nrm(ks[23], (DEPTH, D_MODEL, D_MODEL), D_MODEL ** -0.5),
        'norm_mix_post': gain(ks[24], (DEPTH, D_MODEL)),
        'norm_ffn_pre': gain(ks[25], (DEPTH, D_MODEL)),
        'w_ffn_gate': nrm(ks[26], (DEPTH, D_MODEL, D_FF), D_MODEL ** -0.5),
        'w_ffn_up': nrm(ks[27], (DEPTH, D_MODEL, D_FF), D_MODEL ** -0.5),
        'w_ffn_down': nrm(ks[28], (DEPTH, D_FF, D_MODEL), D_FF ** -0.5),
        'norm_ffn_post': gain(ks[29], (DEPTH, D_MODEL)),
    }


def reference(x_prompt, x_sample, cache_k, cache_v, state_ssm, state_conv, page_table, rel_bias,
              norm_mix_pre, w_in, lambda_q1, lambda_k1, lambda_q2, lambda_k2, attn_subln, conv_w, conv_b,
              dt_bias, a_log, d_skip, ssm_norm, w_br_a, w_br_b, w_out, norm_mix_post, norm_ffn_pre,
              w_ffn_gate, w_ffn_up, w_ffn_down, norm_ffn_post):
    n_seq, n_pages = page_table.shape
    past_len = n_pages * cache_k.shape[2]
    b_p = x_prompt.shape[0]
    zero_conv = jnp.zeros((b_p, CONV_W - 1, CONV_DIM), x_prompt.dtype)
    zero_ssm = jnp.zeros((b_p, N_HEADS_B, HEADDIM_B, D_STATE), jnp.float32)
    hp, hs = x_prompt, x_sample
    kp, vp, sp, cp, ks_, vs_, ss_, cs_ = [], [], [], [], [], [], [], []
    for l in range(DEPTH):
        lam_init = 0.8 - 0.6 * math.exp(-0.3 * l)
        lam = diff_lambda(lambda_q1[l], lambda_k1[l], lambda_q2[l], lambda_k2[l], lam_init)
        lw = (norm_mix_pre[l], w_in[l], attn_subln[l], conv_w[l], conv_b[l], dt_bias[l], a_log[l],
              d_skip[l], ssm_norm[l], w_br_a[l], w_br_b[l], w_out[l], norm_mix_post[l],
              norm_ffn_pre[l], w_ffn_gate[l], w_ffn_up[l], w_ffn_down[l], norm_ffn_post[l])
        att_p = functools.partial(attend_prompt, lam=lam, rel_bias=rel_bias)
        hp, k_new, v_new, c_new, s_new = decoder_layer(hp, att_p, zero_conv, zero_ssm, lam_init, *lw)
        kp.append(k_new)
        vp.append(v_new)
        cp.append(c_new)
        sp.append(s_new)
        k_past = cache_k[l, page_table].reshape(n_seq, past_len, N_HEADS_A, 2, HD_A)
        v_past = cache_v[l, page_table].reshape(n_seq, past_len, N_HEADS_A, DV_A)
        att_s = functools.partial(attend_sample, k_past=k_past, v_past=v_past, lam=lam, rel_bias=rel_bias)
        hs, k_new, v_new, c_new, s_new = decoder_layer(hs, att_s, state_conv[l], state_ssm[l], lam_init, *lw)
        ks_.append(k_new)
        vs_.append(v_new)
        cs_.append(c_new)
        ss_.append(s_new)
    return (hp, hs, jnp.stack(kp), jnp.stack(vp), jnp.stack(sp), jnp.stack(cp),
            jnp.stack(ks_), jnp.stack(vs_), jnp.stack(ss_), jnp.stack(cs_))
```

```python
import functools
import math

import jax
import jax.numpy as jnp
from jax import lax
from jax.experimental import pallas as pl
from jax.experimental.pallas import tpu as pltpu

F32 = jnp.float32
BF16 = jnp.bfloat16

D_MODEL = 1024
N_HEADS_A = 8
HD_A = 64
DV_A = 128
ATTN_W = N_HEADS_A * DV_A
ATTN_SCALE = HD_A ** -0.5
N_BUCKETS = 32
MAX_EXACT = 16
MAX_DISTANCE = 128
D_INNER = 2048
HEADDIM_B = 64
N_HEADS_B = 32
N_GROUPS_B = 4
HEADS_PER_GROUP = N_HEADS_B // N_GROUPS_B
GROUP_W = D_INNER // N_GROUPS_B
D_STATE = 128
CONV_W = 4
CONV_DIM = D_INNER + 2 * N_GROUPS_B * D_STATE
SSD_ROWS = 128
D_FF = 2816
EPS = 1e-6
LANES = 128
NEG_BIG = -1e30
VMEM_LIMIT = 56 * 1024 * 1024

C_Q, C_K, C_V = 0, 1024, 2048
C_Z = 3072
C_XBC = C_Z + D_INNER
C_GATE = C_XBC + CONV_DIM
C_DT = C_GATE + 2 * D_MODEL
IN_COLS_PAD = C_DT + LANES


def _cparams(sem):
    return pltpu.CompilerParams(dimension_semantics=sem, vmem_limit_bytes=VMEM_LIMIT)


def _resident(shape):
    nd = len(shape)
    return pl.BlockSpec(shape, lambda *_: (0,) * nd, pipeline_mode=pl.Buffered(1))


def _rms(x, g):
    ms = jnp.mean(x * x, axis=-1, keepdims=True)
    return x * lax.rsqrt(ms + EPS) * g


def _silu(x):
    return x * jax.nn.sigmoid(x)


def _dot(a, b):
    return jnp.dot(a, b, preferred_element_type=F32)


def _dot_nt(a, b):
    return lax.dot_general(a, b, (((1,), (1,)), ((), ())), preferred_element_type=F32)


def _dot_tn(a, b):
    return lax.dot_general(a, b, (((0,), (0,)), ((), ())), preferred_element_type=F32)


def _t5_bias(dist, rb_of_bucket):
    dist = jnp.maximum(dist, 0)
    d = jnp.maximum(dist, MAX_EXACT).astype(F32)
    large = MAX_EXACT + (jnp.log(d / MAX_EXACT) / math.log(MAX_DISTANCE / MAX_EXACT)
                         * (N_BUCKETS - MAX_EXACT)).astype(jnp.int32)
    large = jnp.minimum(large, N_BUCKETS - 1)
    bucket = jnp.where(dist < MAX_EXACT, dist, large)
    far = rb_of_bucket(N_BUCKETS - 1)
    out = jnp.zeros(dist.shape, F32)
    for j in range(N_BUCKETS - 1):
        out = jnp.where(bucket == j, rb_of_bucket(j) - far, out)
    return out


def _diff_lambda(lq1, lk1, lq2, lk2, lam_init):
    s1 = jnp.sum(lq1[...] * lk1[...], axis=-1, keepdims=True)
    s2 = jnp.sum(lq2[...] * lk2[...], axis=-1, keepdims=True)
    return jnp.exp(s1) - jnp.exp(s2) + lam_init


def _proj_kernel(h_ref, g_ref, w_ref, q_ref, k_ref, v_ref, kb_ref, vb_ref, z_ref, xbc_ref,
                 gate_ref, dt_ref):
    u = _rms(h_ref[...], g_ref[...]).astype(BF16)
    step = 512

    def cols(c0, n):
        for c in range(0, n, step):
            w = min(step, n - c)
            yield c, w, _dot(u, w_ref[:, c0 + c:c0 + c + w])

    for c, w, r in cols(C_Q, ATTN_W):
        q_ref[:, c:c + w] = r.astype(BF16)
    for c, w, r in cols(C_K, ATTN_W):
        k_ref[:, c:c + w] = r
        kb_ref[:, c:c + w] = r.astype(BF16)
    for c, w, r in cols(C_V, ATTN_W):
        v_ref[:, c:c + w] = r
        vb_ref[:, c:c + w] = r.astype(BF16)
    for c, w, r in cols(C_Z, D_INNER):
        z_ref[:, c:c + w] = r
    for c, w, r in cols(C_XBC, CONV_DIM):
        xbc_ref[:, c:c + w] = r
    for c, w, r in cols(C_GATE, 2 * D_MODEL):
        gate_ref[:, c:c + w] = r
    for c, w, r in cols(C_DT, LANES):
        dt_ref[:, c:c + w] = r


def _proj(h, gain, w):
    t = h.shape[0]
    tm = min(256, t)
    assert t % tm == 0
    row = lambda n: pl.BlockSpec((tm, n), lambda i: (i, 0))
    out_shape = (
        jax.ShapeDtypeStruct((t, ATTN_W), BF16),
        jax.ShapeDtypeStruct((t, ATTN_W), F32),
        jax.ShapeDtypeStruct((t, ATTN_W), F32),
        jax.ShapeDtypeStruct((t, ATTN_W), BF16),
        jax.ShapeDtypeStruct((t, ATTN_W), BF16),
        jax.ShapeDtypeStruct((t, D_INNER), F32),
        jax.ShapeDtypeStruct((t, CONV_DIM), F32),
        jax.ShapeDtypeStruct((t, 2 * D_MODEL), F32),
        jax.ShapeDtypeStruct((t, LANES), F32),
    )
    return pl.pallas_call(
        _proj_kernel,
        grid=(t // tm,),
        in_specs=[row(D_MODEL), _resident((1, D_MODEL)), _resident((D_MODEL, IN_COLS_PAD))],
        out_specs=tuple(row(s.shape[1]) for s in out_shape),
        out_shape=out_shape,
        compiler_params=_cparams(("parallel",)),
        name="proj",
    )(h, gain, w)


def _attn_prompt_kernel(rb_ref, lq1, lk1, lq2, lk2, subln_ref, q_ref, k_ref, v_ref, o_ref,
                        qs_ref, m_ref, l_ref, acc_ref, bias_ref, *, tq, tk, lam_init):
    h = pl.program_id(1)
    i = pl.program_id(2)

    @pl.when(i == 0)
    def _build_bias():
        r = lax.broadcasted_iota(jnp.int32, (tq, tk), 0)
        c = lax.broadcasted_iota(jnp.int32, (tq, tk), 1)
        for which in range(2):
            dist = r - c + which * tk
            b = _t5_bias(dist, lambda j: rb_ref[j, h])
            if which == 0:
                b = jnp.where(dist >= 0, b, NEG_BIG)
            bias_ref[which] = b

    q = q_ref[...]
    lane = lax.broadcasted_iota(jnp.int32, q.shape, 1)
    zero = jnp.zeros_like(q)
    qs_ref[0:tq, :] = jnp.where(lane < HD_A, q, zero)
    qs_ref[tq:2 * tq, :] = jnp.where(lane >= HD_A, q, zero)
    m_ref[...] = jnp.full(m_ref.shape, NEG_BIG, F32)
    l_ref[...] = jnp.zeros(l_ref.shape, F32)
    acc_ref[...] = jnp.zeros(acc_ref.shape, F32)

    def update(j, bias):
        start = pl.multiple_of(j * tk, tk)
        kt = k_ref[pl.ds(start, tk), :]
        vt = v_ref[pl.ds(start, tk), :]
        s = _dot_nt(qs_ref[...], kt)
        if bias is not None:
            s = (s.reshape(2, tq, tk) + bias[None]).reshape(2 * tq, tk)
        m_prev = m_ref[...]
        m_new = jnp.maximum(m_prev, jnp.max(s, axis=1, keepdims=True))
        alpha = jnp.exp(m_prev - m_new)
        p = jnp.exp(s - jnp.tile(m_new, (1, tk // LANES)))
        l_ref[...] = alpha * l_ref[...] + jnp.sum(p, axis=1, keepdims=True)
        m_ref[...] = m_new
        acc_ref[...] = alpha * acc_ref[...] + _dot(p.astype(BF16), vt)

    def far_body(j, carry):
        update(j, None)
        return carry

    lax.fori_loop(0, jnp.maximum(i - 1, 0), far_body, 0)

    @pl.when(i >= 1)
    def _prev():
        update(i - 1, bias_ref[1])

    update(i, bias_ref[0])

    lam = _diff_lambda(lq1, lk1, lq2, lk2, lam_init)
    o = acc_ref[...] / l_ref[...]
    o = o[0:tq] - lam * o[tq:2 * tq]
    o = _rms(o, subln_ref[...]) * (1.0 - lam_init)
    o_ref[...] = o.astype(o_ref.dtype)


def _attn_prompt(q, k, v, rel_bias, lams, subln, lam_init):
    b, l, _ = q.shape
    tq = tk = 256
    assert l % tq == 0 and tk >= MAX_DISTANCE
    vec = lambda n: pl.BlockSpec((1, n), lambda b_, h_, i_: (0, 0))
    kern = functools.partial(_attn_prompt_kernel, tq=tq, tk=tk, lam_init=lam_init)
    return pl.pallas_call(
        kern,
        grid=(b, N_HEADS_A, l // tq),
        in_specs=[
            pl.BlockSpec(memory_space=pltpu.SMEM),
            vec(HD_A), vec(HD_A), vec(HD_A), vec(HD_A), vec(DV_A),
            pl.BlockSpec((None, tq, DV_A), lambda b_, h_, i_: (b_, i_, h_)),
            pl.BlockSpec((None, l, DV_A), lambda b_, h_, i_: (b_, 0, h_)),
            pl.BlockSpec((None, l, DV_A), lambda b_, h_, i_: (b_, 0, h_)),
        ],
        out_specs=pl.BlockSpec((None, tq, DV_A), lambda b_, h_, i_: (b_, i_, h_)),
        out_shape=jax.ShapeDtypeStruct((b, l, ATTN_W), BF16),
        scratch_shapes=[
            pltpu.VMEM((2 * tq, DV_A), BF16),
            pltpu.VMEM((2 * tq, LANES), F32),
            pltpu.VMEM((2 * tq, LANES), F32),
            pltpu.VMEM((2 * tq, DV_A), F32),
            pltpu.VMEM((2, tq, tk), F32),
        ],
        compiler_params=_cparams(("parallel", "parallel", "arbitrary")),
        name="attn_prompt",
    )(rel_bias, *lams, subln, q, k, v)


def _attn_sample_kernel(pt_ref, rbx_ref, lq1, lk1, lq2, lk2, subln_ref, qt_ref, kc_ref, vc_ref,
                        kn_ref, vn_ref, o_ref, m_ref, l_ref, acc_ref, bias_ref, *,
                        page, n_new, lam_init):
    del pt_ref
    b = pl.program_id(0)
    p = pl.program_id(1)
    n_pages = pl.num_programs(1)
    nh = N_HEADS_A

    def col_head(col):
        return (col % 64) // n_new

    @pl.when((b == 0) & (p == 0))
    def _build_bias():
        shape = (page, nh, LANES)
        tok = lax.broadcasted_iota(jnp.int32, shape, 0)
        col = lax.broadcasted_iota(jnp.int32, shape, 2)
        dist = page + col % n_new - tok
        bias_ref[...] = _t5_bias(dist, lambda j: rbx_ref[j:j + 1, :][None])

    @pl.when(p == 0)
    def _init():
        m_ref[...] = jnp.full(m_ref.shape, NEG_BIG, F32)
        l_ref[...] = jnp.zeros(l_ref.shape, F32)
        acc_ref[...] = jnp.zeros(acc_ref.shape, F32)

    qt = qt_ref[...]

    def update(k2, v2, extra, n_tok):
        s = _dot(k2.astype(BF16), qt).reshape(n_tok, nh, LANES) + extra
        m_prev = m_ref[...]
        m_cur = jnp.max(jnp.max(s, axis=0), axis=0, keepdims=True)
        m_new = jnp.maximum(m_prev, m_cur)
        alpha = jnp.exp(m_prev - m_new)
        pr = jnp.exp(s - m_new[None])
        l_ref[...] = alpha * l_ref[...] + jnp.sum(jnp.sum(pr, axis=0), axis=0, keepdims=True)
        m_ref[...] = m_new
        pv = _dot_tn(v2.astype(BF16), pr.reshape(n_tok * nh, LANES).astype(BF16))
        acc_ref[...] = alpha * acc_ref[...] + pv

    sub = lax.broadcasted_iota(jnp.int32, (nh, LANES), 0)
    col = lax.broadcasted_iota(jnp.int32, (nh, LANES), 1)
    head_mask = jnp.where(col_head(col) == sub, 0.0, NEG_BIG).astype(F32)

    k2 = kc_ref[...].reshape(page * nh, DV_A)
    v2 = vc_ref[...].reshape(page * nh, DV_A)

    @pl.when(p < n_pages - 1)
    def _far_page():
        update(k2, v2, head_mask[None], page)

    @pl.when(p == n_pages - 1)
    def _last_page_and_new_tokens():
        update(k2, v2, head_mask[None] + bias_ref[...], page)
        shape = (n_new, nh, LANES)
        tok = lax.broadcasted_iota(jnp.int32, shape, 0)
        c3 = lax.broadcasted_iota(jnp.int32, shape, 2)
        dist = c3 % n_new - tok
        bias_new = _t5_bias(dist, lambda j: rbx_ref[j:j + 1, :][None])
        extra = jnp.where(dist >= 0, bias_new, NEG_BIG) + head_mask[None]
        update(kn_ref[...].reshape(n_new * nh, DV_A), vn_ref[...].reshape(n_new * nh, DV_A),
               extra, n_new)
        lam = _diff_lambda(lq1, lk1, lq2, lk2, lam_init)
        on = acc_ref[...] / l_ref[...]
        o = on - lam * pltpu.roll(on, 64, 1)
        ms = jnp.mean(o * o, axis=0, keepdims=True)
        o_ref[...] = o * lax.rsqrt(ms + EPS) * subln_ref[...] * (1.0 - lam_init)


def _attn_sample(layer, q, k_new, v_new, cache_k, cache_v, page_table, rel_bias, lams, subln,
                 lam_init):
    s, n_new, _ = q.shape
    n_pages = page_table.shape[1]
    page = cache_k.shape[2]
    assert n_new == 8 and page >= MAX_DISTANCE and page % 8 == 0
    qt = jnp.transpose(q.reshape(s, n_new, N_HEADS_A, DV_A), (0, 3, 2, 1)).reshape(s, DV_A, 64)
    d_idx = jnp.arange(DV_A)[None, :, None]
    zero = jnp.zeros_like(qt)
    qt = jnp.concatenate([jnp.where(d_idx < HD_A, qt, zero), jnp.where(d_idx >= HD_A, qt, zero)], -1)
    rbx = jnp.tile(jnp.repeat(rel_bias, n_new, axis=1), (1, 2))
    subln_col = jnp.broadcast_to(subln.reshape(DV_A, 1), (DV_A, LANES))
    vec = lambda n: pl.BlockSpec((1, n), lambda b_, p_, pt: (0, 0))
    kern = functools.partial(_attn_sample_kernel, page=page, n_new=n_new, lam_init=lam_init)
    grid_spec = pltpu.PrefetchScalarGridSpec(
        num_scalar_prefetch=1,
        grid=(s, n_pages),
        in_specs=[
            pl.BlockSpec((N_BUCKETS, LANES), lambda b_, p_, pt: (0, 0)),
            vec(HD_A), vec(HD_A), vec(HD_A), vec(HD_A),
            pl.BlockSpec((DV_A, LANES), lambda b_, p_, pt: (0, 0)),
            pl.BlockSpec((None, DV_A, LANES), lambda b_, p_, pt: (b_, 0, 0)),
            pl.BlockSpec((None, None, page, N_HEADS_A, DV_A),
                         lambda b_, p_, pt: (layer, pt[b_, p_], 0, 0, 0)),
            pl.BlockSpec((None, None, page, N_HEADS_A, DV_A),
                         lambda b_, p_, pt: (layer, pt[b_, p_], 0, 0, 0)),
            pl.BlockSpec((None, n_new, N_HEADS_A, DV_A), lambda b_, p_, pt: (b_, 0, 0, 0)),
            pl.BlockSpec((None, n_new, N_HEADS_A, DV_A), lambda b_, p_, pt: (b_, 0, 0, 0)),
        ],
        out_specs=pl.BlockSpec((None, DV_A, LANES), lambda b_, p_, pt: (b_, 0, 0)),
        scratch_shapes=[
            pltpu.VMEM((1, LANES), F32),
            pltpu.VMEM((1, LANES), F32),
            pltpu.VMEM((DV_A, LANES), F32),
            pltpu.VMEM((page, N_HEADS_A, LANES), F32),
        ],
    )
    o = pl.pallas_call(
        kern,
        grid_spec=grid_spec,
        out_shape=jax.ShapeDtypeStruct((s, DV_A, LANES), F32),
        compiler_params=_cparams(("arbitrary", "arbitrary")),
        name="attn_sample",
    )(page_table, rbx, *lams, subln_col, qt, cache_k, cache_v, k_new, v_new)
    o = o[:, :, :64].reshape(s, DV_A, N_HEADS_A, n_new)
    return jnp.transpose(o, (0, 3, 2, 1)).reshape(s * n_new, ATTN_W).astype(BF16)


def _ssd_prepare(xp_rows, dtr, cw_ref, cb_ref, dtb_ref, alog_ref, dsk_ref, seq_len,
                 xs_s, bm_s, cm_s, dt_s, acs_s, aend_s, acst_s, dsk_s):
    r = SSD_ROWS
    y = cb_ref[...] + xp_rows(0) * cw_ref[0:1, :]
    for j in range(1, CONV_W):
        y = y + xp_rows(j) * cw_ref[j:j + 1, :]
    xc = _silu(y)
    dt = jax.nn.softplus(dtr + dtb_ref[...])
    a = -jnp.exp(alog_ref[...])
    dta = dt * a
    row = lax.broadcasted_iota(jnp.int32, (r, r), 0)
    col = lax.broadcasted_iota(jnp.int32, (r, r), 1)
    same = (row // seq_len) == (col // seq_len)
    tri = (same & (col <= row)).astype(F32)
    a_cs = jnp.dot(tri, dta, precision=lax.Precision.HIGHEST, preferred_element_type=F32)
    a_end = jnp.dot(same.astype(F32), dta, precision=lax.Precision.HIGHEST,
                    preferred_element_type=F32)
    a_cs_t = a_cs.T
    dsk = jnp.broadcast_to(dsk_ref[...], (8, LANES))
    for g in range(N_GROUPS_B):
        xs_s[g] = xc[:, g * GROUP_W:(g + 1) * GROUP_W]
        bm_s[g] = xc[:, D_INNER + g * D_STATE:D_INNER + (g + 1) * D_STATE].astype(BF16)
        c0 = D_INNER + N_GROUPS_B * D_STATE + g * D_STATE
        cm_s[g] = xc[:, c0:c0 + D_STATE].astype(BF16)
        shift = (LANES - HEADS_PER_GROUP * g) % LANES
        roll = (lambda x: x) if shift == 0 else (lambda x: pltpu.roll(x, shift, 1))
        dt_s[g] = roll(dt)
        acs_s[g] = roll(a_cs)
        aend_s[g] = roll(a_end)
        dsk_s[g] = roll(dsk)
        acst_s[g] = a_cs_t[g * HEADS_PER_GROUP:(g + 1) * HEADS_PER_GROUP, :]


def _ssd_group(g, seq_len, z_g, nw_g, state_t, xs_s, bm_s, cm_s, dt_s, acs_s, aend_s, acst_s,
               dsk_s):
    r = SSD_ROWS
    n_seq = r // seq_len
    xs_g = xs_s[g]
    bmb = bm_s[g]
    cmb = cm_s[g]
    dtg = dt_s[g]
    acs = acs_s[g]
    aend = aend_s[g]
    acst = acst_s[g]
    dsk = dsk_s[g]
    exp_acs = jnp.exp(acs)
    to_end = jnp.exp(aend - acs)
    cdec = jnp.exp(aend)

    row = lax.broadcasted_iota(jnp.int32, (r, r), 0)
    col = lax.broadcasted_iota(jnp.int32, (r, r), 1)
    tri = ((row // seq_len) == (col // seq_len)) & (col <= row)
    row_seq = lax.broadcasted_iota(jnp.int32, (r, 1), 0) // seq_len

    cb = _dot_nt(cmb, bmb)

    if n_seq == 1:
        y_off = _dot(cmb, state_t[0].astype(BF16))
    else:
        y_off = jnp.zeros((r, GROUP_W), F32)
        for j in range(n_seq):
            y_off = y_off + jnp.where(row_seq == j, _dot(cmb, state_t[j].astype(BF16)), 0.0)

    left = lax.broadcasted_iota(jnp.int32, (1, LANES), 1) < HEADDIM_B
    gz_parts, xw_parts, dec_parts = [], [], []

    for pp in range(HEADS_PER_GROUP // 2):
        h0, h1 = 2 * pp, 2 * pp + 1
        lo, hi = pp * LANES, (pp + 1) * LANES

        def expand2(arr):
            return jnp.where(left, arr[:, h0:h0 + 1], arr[:, h1:h1 + 1])

        xs_p = xs_g[:, lo:hi]
        xdt = xs_p * expand2(dtg)
        xdt_b = xdt.astype(BF16)
        zero_b = jnp.zeros_like(xdt_b)
        y = y_off[:, lo:hi] * expand2(exp_acs) + expand2(dsk[0:1, :]) * xs_p
        for hh, keep in ((h0, left), (h1, jnp.logical_not(left))):
            seg = acs[:, hh:hh + 1] - acst[hh:hh + 1, :]
            decay = jnp.exp(jnp.where(tri, seg, -jnp.inf))
            gm = (cb * decay).astype(BF16)
            y = y + _dot(gm, jnp.where(keep, xdt_b, zero_b))
        gz_parts.append(y * _silu(z_g[:, lo:hi]))
        xw_parts.append((xdt * expand2(to_end)).astype(BF16))
        dec_parts.append(expand2(cdec))

    gz = jnp.concatenate(gz_parts, axis=1)
    ms = jnp.mean(gz * gz, axis=-1, keepdims=True)
    out = gz * lax.rsqrt(ms + EPS) * nw_g

    xw = jnp.concatenate(xw_parts, axis=1)
    dec = jnp.concatenate(dec_parts, axis=1)
    new_states = []
    for j in range(n_seq):
        if n_seq == 1:
            bsel = bmb
        else:
            bsel = jnp.where(row_seq == j, bmb, jnp.zeros_like(bmb))
        new_states.append(state_t[j] * dec[j * seq_len:j * seq_len + 1, :] + _dot_tn(bsel, xw))
    return out, new_states


_SSD_SCRATCH = [
    pltpu.VMEM((N_GROUPS_B, SSD_ROWS, GROUP_W), F32),
    pltpu.VMEM((N_GROUPS_B, SSD_ROWS, D_STATE), BF16),
    pltpu.VMEM((N_GROUPS_B, SSD_ROWS, D_STATE), BF16),
    pltpu.VMEM((N_GROUPS_B, SSD_ROWS, LANES), F32),
    pltpu.VMEM((N_GROUPS_B, SSD_ROWS, LANES), F32),
    pltpu.VMEM((N_GROUPS_B, SSD_ROWS, LANES), F32),
    pltpu.VMEM((N_GROUPS_B, HEADS_PER_GROUP, SSD_ROWS), F32),
    pltpu.VMEM((N_GROUPS_B, 8, LANES), F32),
]


def _ssd_prompt_kernel(cur_ref, hist_ref, dtr_ref, z_ref, cw_ref, cb_ref, dtb_ref, alog_ref,
                       dsk_ref, nw_ref, o_ref, st_ref, xp_s, ht_s, *scr):
    c = pl.program_id(1)
    n_c = pl.num_programs(1)
    r = SSD_ROWS

    @pl.when(c == 0)
    def _start():
        ht_s[...] = jnp.zeros(ht_s.shape, F32)
        xp_s[0:8, :] = jnp.zeros((8, CONV_DIM), F32)

    @pl.when(c > 0)
    def _history():
        xp_s[0:8, :] = hist_ref[...]

    xp_s[8:8 + r, :] = cur_ref[...]
    xp_rows = lambda j: xp_s[pl.ds(8 - (CONV_W - 1) + j, r), :]
    _ssd_prepare(xp_rows, dtr_ref[...], cw_ref, cb_ref, dtb_ref, alog_ref, dsk_ref, r, *scr[:8])
    for g in range(N_GROUPS_B):
        out, new = _ssd_group(g, r, z_ref[:, g * GROUP_W:(g + 1) * GROUP_W], nw_ref[g],
                              [ht_s[g]], *scr)
        o_ref[:, g * GROUP_W:(g + 1) * GROUP_W] = out.astype(o_ref.dtype)
        ht_s[g] = new[0]

    @pl.when(c == n_c - 1)
    def _emit_state():
        for g in range(N_GROUPS_B):
            st_ref[g * GROUP_W:(g + 1) * GROUP_W, :] = ht_s[g].T


def _ssd_prompt(xbc, dtr, z, cw, cb, dtb, alog, dsk, nw):
    b, l, _ = xbc.shape
    r = SSD_ROWS
    assert l % r == 0
    tile = lambda n: pl.BlockSpec((None, r, n), lambda b_, c_: (b_, c_, 0))
    full = lambda shape: pl.BlockSpec(shape, lambda b_, c_: (0,) * len(shape))
    return pl.pallas_call(
        _ssd_prompt_kernel,
        grid=(b, l // r),
        in_specs=[
            tile(CONV_DIM),
            pl.BlockSpec((None, 8, CONV_DIM), lambda b_, c_: (b_, jnp.maximum(c_ * (r // 8) - 1, 0), 0)),
            tile(LANES), tile(D_INNER),
            full((CONV_W, CONV_DIM)), full((1, CONV_DIM)), full((1, LANES)), full((1, LANES)),
            full((1, LANES)), full((N_GROUPS_B, 1, GROUP_W)),
        ],
        out_specs=(tile(D_INNER), pl.BlockSpec((None, D_INNER, D_STATE), lambda b_, c_: (b_, 0, 0))),
        out_shape=(jax.ShapeDtypeStruct((b, l, D_INNER), BF16),
                   jax.ShapeDtypeStruct((b, D_INNER, D_STATE), F32)),
        scratch_shapes=[pltpu.VMEM((8 + r, CONV_DIM), F32),
                        pltpu.VMEM((N_GROUPS_B, D_STATE, GROUP_W), F32)] + _SSD_SCRATCH,
        compiler_params=_cparams(("parallel", "arbitrary")),
        name="ssd_prompt",
    )(xbc, xbc, dtr, z, cw, cb, dtb, alog, dsk, nw)


def _ssd_sample_kernel(xp_ref, dtr_ref, z_ref, cw_ref, cb_ref, dtb_ref, alog_ref, dsk_ref, nw_ref,
                       h0_ref, o_ref, st_ref, *scr, seq_len):
    g = pl.program_id(1)
    r = SSD_ROWS
    n_seq = r // seq_len

    @pl.when(g == 0)
    def _prepare():
        xp_rows = lambda j: xp_ref[:, pl.ds(j, seq_len), :].reshape(r, CONV_DIM)
        _ssd_prepare(xp_rows, dtr_ref[...], cw_ref, cb_ref, dtb_ref, alog_ref, dsk_ref, seq_len,
                     *scr[:8])

    states = [h0_ref[j].reshape(GROUP_W, D_STATE).T for j in range(n_seq)]
    out, new = _ssd_group(g, seq_len, z_ref[...], nw_ref[...], states, *scr)
    o_ref[...] = out.astype(o_ref.dtype)
    for j in range(n_seq):
        st_ref[j] = new[j].T.reshape(HEADS_PER_GROUP, HEADDIM_B, D_STATE)


def _ssd_sample(layer, xp, dtr, z, state_ssm, cw, cb, dtb, alog, dsk, nw):
    s, win, _ = xp.shape
    seq_len = win - (CONV_W - 1)
    r = SSD_ROWS
    n_seq = r // seq_len
    assert seq_len == 8 and s % n_seq == 0
    full = lambda shape: pl.BlockSpec(shape, lambda t_, g_: (0,) * len(shape))
    kern = functools.partial(_ssd_sample_kernel, seq_len=seq_len)
    return pl.pallas_call(
        kern,
        grid=(s // n_seq, N_GROUPS_B),
        in_specs=[
            pl.BlockSpec((n_seq, win, CONV_DIM), lambda t_, g_: (t_, 0, 0)),
            pl.BlockSpec((r, LANES), lambda t_, g_: (t_, 0)),
            pl.BlockSpec((r, GROUP_W), lambda t_, g_: (t_, g_)),
            full((CONV_W, CONV_DIM)), full((1, CONV_DIM)), full((1, LANES)), full((1, LANES)),
            full((1, LANES)),
            pl.BlockSpec((None, 1, GROUP_W), lambda t_, g_: (g_, 0, 0)),
            pl.BlockSpec((None, n_seq, HEADS_PER_GROUP, HEADDIM_B, D_STATE),
                         lambda t_, g_: (layer, t_, g_, 0, 0)),
        ],
        out_specs=(pl.BlockSpec((r, GROUP_W), lambda t_, g_: (t_, g_)),
                   pl.BlockSpec((n_seq, HEADS_PER_GROUP, HEADDIM_B, D_STATE),
                                lambda t_, g_: (t_, g_, 0, 0))),
        out_shape=(jax.ShapeDtypeStruct((s * seq_len, D_INNER), BF16),
                   jax.ShapeDtypeStruct((s, N_HEADS_B, HEADDIM_B, D_STATE), F32)),
        scratch_shapes=_SSD_SCRATCH,
        compiler_params=_cparams(("parallel", "arbitrary")),
        name="ssd_sample",
    )(xp, dtr, z, cw, cb, dtb, alog, dsk, nw, state_ssm)


def _post_kernel(h_ref, a_ref, s_ref, gate_ref, wa_ref, wb_ref, wo_ref, n1_ref, n2_ref, wg_ref,
                 wu_ref, wd_ref, n3_ref, o_ref):
    ya = _dot(a_ref[...], wa_ref[...])
    yb = _dot(s_ref[...], wb_ref[...])
    merged = (jax.nn.sigmoid(gate_ref[:, 0:D_MODEL]) * ya
              + jax.nn.sigmoid(gate_ref[:, D_MODEL:2 * D_MODEL]) * yb)
    mix = _dot(merged.astype(BF16), wo_ref[...])
    h1 = h_ref[...] + _rms(mix, n1_ref[...])
    u = _rms(h1, n2_ref[...]).astype(BF16)
    ff = jnp.zeros(h1.shape, F32)
    step = 256
    for c in range(0, D_FF, step):
        hid = _silu(_dot(u, wg_ref[:, c:c + step])) * _dot(u, wu_ref[:, c:c + step])
        ff = ff + _dot(hid.astype(BF16), wd_ref[c:c + step, :])
    o_ref[...] = h1 + _rms(ff, n3_ref[...])


def _post(h, attn_o, ssm_o, gates, wa, wb, wo, n1, n2, wg, wu, wd, n3):
    t = h.shape[0]
    tm = min(256, t)
    assert t % tm == 0
    row = lambda n: pl.BlockSpec((tm, n), lambda i: (i, 0))
    return pl.pallas_call(
        _post_kernel,
        grid=(t // tm,),
        in_specs=[row(D_MODEL), row(ATTN_W), row(D_INNER), row(2 * D_MODEL),
                  _resident(wa.shape), _resident(wb.shape), _resident(wo.shape),
                  _resident((1, D_MODEL)), _resident((1, D_MODEL)),
                  _resident(wg.shape), _resident(wu.shape), _resident(wd.shape),
                  _resident((1, D_MODEL))],
        out_specs=row(D_MODEL),
        out_shape=jax.ShapeDtypeStruct((t, D_MODEL), F32),
        compiler_params=_cparams(("parallel",)),
        name="post",
    )(h, attn_o, ssm_o, gates, wa, wb, wo, n1, n2, wg, wu, wd, n3)


def _pad_lanes(x):
    return jnp.pad(x.reshape(1, -1), ((0, 0), (0, LANES - x.shape[-1])))


def kernel(x_prompt, x_sample, cache_k, cache_v, state_ssm, state_conv, page_table, rel_bias,
           norm_mix_pre, w_in, lambda_q1, lambda_k1, lambda_q2, lambda_k2, attn_subln, conv_w,
           conv_b, dt_bias, a_log, d_skip, ssm_norm, w_br_a, w_br_b, w_out, norm_mix_post,
           norm_ffn_pre, w_ffn_gate, w_ffn_up, w_ffn_down, norm_ffn_post):
    bsz, seq, _ = x_prompt.shape
    n_smp, n_new, _ = x_sample.shape
    depth = w_in.shape[0]
    hp = x_prompt.reshape(bsz * seq, D_MODEL)
    hs = x_sample.reshape(n_smp * n_new, D_MODEL)
    row = lambda x: x.reshape(1, -1)
    outs = [[] for _ in range(8)]
    for l in range(depth):
        lam_init = 0.8 - 0.6 * math.exp(-0.3 * l)
        wl = w_in[l]
        w_proj = jnp.concatenate([
            wl[:, 0:ATTN_W] * ATTN_SCALE,
            wl[:, ATTN_W:8192], wl[:, 8224:10272], wl[:, 8192:8224],
            jnp.zeros((D_MODEL, LANES - N_HEADS_B), F32)], axis=1).astype(BF16)
        lams = (row(lambda_q1[l]), row(lambda_k1[l]), row(lambda_q2[l]), row(lambda_k2[l]))
        ssd_w = (conv_w[l], row(conv_b[l]), _pad_lanes(dt_bias[l]), _pad_lanes(a_log[l]),
                 _pad_lanes(d_skip[l]), ssm_norm[l].reshape(N_GROUPS_B, 1, GROUP_W))
        post_w = (w_br_a[l].astype(BF16), w_br_b[l].astype(BF16), w_out[l].astype(BF16),
                  row(norm_mix_post[l]), row(norm_ffn_pre[l]), w_ffn_gate[l].astype(BF16),
                  w_ffn_up[l].astype(BF16), w_ffn_down[l].astype(BF16), row(norm_ffn_post[l]))
        subln = row(attn_subln[l])

        q, k, v, kb, vb, z, xbc, gates, dtr = _proj(hp, row(norm_mix_pre[l]), w_proj)
        b3 = lambda x: x.reshape(bsz, seq, x.shape[-1])
        attn_o = _attn_prompt(b3(q), b3(kb), b3(vb), rel_bias, lams, subln, lam_init)
        ssm_o, st = _ssd_prompt(b3(xbc), b3(dtr), b3(z), *ssd_w)
        hp = _post(hp, attn_o.reshape(bsz * seq, ATTN_W), ssm_o.reshape(bsz * seq, D_INNER), gates,
                   *post_w)
        outs[0].append(k.reshape(bsz, seq, N_HEADS_A, DV_A))
        outs[1].append(v.reshape(bsz, seq, N_HEADS_A, DV_A))
        outs[2].append(st.reshape(bsz, N_HEADS_B, HEADDIM_B, D_STATE))
        outs[3].append(b3(xbc)[:, seq - (CONV_W - 1):, :])

        q, k, v, kb, vb, z, xbc, gates, dtr = _proj(hs, row(norm_mix_pre[l]), w_proj)
        k4 = k.reshape(n_smp, n_new, N_HEADS_A, DV_A)
        v4 = v.reshape(n_smp, n_new, N_HEADS_A, DV_A)
        attn_o = _attn_sample(l, q.reshape(n_smp, n_new, ATTN_W), k4, v4, cache_k, cache_v,
                              page_table, rel_bias, lams, subln, lam_init)
        xp = jnp.concatenate([state_conv[l], xbc.reshape(n_smp, n_new, CONV_DIM)], axis=1)
        ssm_o, st = _ssd_sample(l, xp, dtr, z, state_ssm, *ssd_w)
        hs = _post(hs, attn_o, ssm_o, gates, *post_w)
        outs[4].append(k4)
        outs[5].append(v4)
        outs[6].append(st)
        outs[7].append(xp[:, n_new:, :])

    stk = [jnp.stack(o) for o in outs]
    return (hp.reshape(bsz, seq, D_MODEL), hs.reshape(n_smp, n_new, D_MODEL),
            stk[0], stk[1], stk[2], stk[3], stk[4], stk[5], stk[6], stk[7])
```

```python
import functools
import math

import jax
import jax.numpy as jnp
from jax import lax
from jax.experimental import pallas as pl
from jax.experimental.pallas import tpu as pltpu

F32 = jnp.float32
BF16 = jnp.bfloat16

D_MODEL = 1024
N_HEADS_A = 8
HD_A = 64
DV_A = 128
ATTN_W = N_HEADS_A * DV_A
ATTN_SCALE = HD_A ** -0.5
N_BUCKETS = 32
MAX_EXACT = 16
MAX_DISTANCE = 128
D_INNER = 2048
HEADDIM_B = 64
N_HEADS_B = 32
N_GROUPS_B = 4
HEADS_PER_GROUP = N_HEADS_B // N_GROUPS_B
GROUP_W = D_INNER // N_GROUPS_B
D_STATE = 128
CONV_W = 4
CONV_DIM = D_INNER + 2 * N_GROUPS_B * D_STATE
SSD_ROWS = 128
D_FF = 2816
EPS = 1e-6
LANES = 128
NEG_BIG = -1e30
VMEM_LIMIT = 56 * 1024 * 1024

C_Q, C_K, C_V = 0, 1024, 2048
C_Z = 3072
C_XBC = C_Z + D_INNER
C_GATE = C_XBC + CONV_DIM
C_DT = C_GATE + 2 * D_MODEL
IN_COLS_PAD = C_DT + LANES


def _cparams(sem):
    return pltpu.CompilerParams(dimension_semantics=sem, vmem_limit_bytes=VMEM_LIMIT)


def _resident(shape):
    nd = len(shape)
    return pl.BlockSpec(shape, lambda *_: (0,) * nd, pipeline_mode=pl.Buffered(1))


def _rms(x, g):
    ms = jnp.mean(x * x, axis=-1, keepdims=True)
    return x * lax.rsqrt(ms + EPS) * g


def _silu(x):
    return x * jax.nn.sigmoid(x)


def _dot(a, b):
    return jnp.dot(a, b, preferred_element_type=F32)


def _dot_nt(a, b):
    return lax.dot_general(a, b, (((1,), (1,)), ((), ())), preferred_element_type=F32)


def _dot_tn(a, b):
    return lax.dot_general(a, b, (((0,), (0,)), ((), ())), preferred_element_type=F32)


def _t5_bias(dist, rb_of_bucket):
    dist = jnp.maximum(dist, 0)
    d = jnp.maximum(dist, MAX_EXACT).astype(F32)
    large = MAX_EXACT + (jnp.log(d / MAX_EXACT) / math.log(MAX_DISTANCE / MAX_EXACT)
                         * (N_BUCKETS - MAX_EXACT)).astype(jnp.int32)
    large = jnp.minimum(large, N_BUCKETS - 1)
    bucket = jnp.where(dist < MAX_EXACT, dist, large)
    far = rb_of_bucket(N_BUCKETS - 1)
    out = jnp.zeros(dist.shape, F32)
    for j in range(N_BUCKETS - 1):
        out = jnp.where(bucket == j, rb_of_bucket(j) - far, out)
    return out


def _diff_lambda(lq1, lk1, lq2, lk2, lam_init):
    s1 = jnp.sum(lq1[...] * lk1[...], axis=-1, keepdims=True)
    s2 = jnp.sum(lq2[...] * lk2[...], axis=-1, keepdims=True)
    return jnp.exp(s1) - jnp.exp(s2) + lam_init


def _proj_kernel(h_ref, g_ref, w_ref, q_ref, k_ref, v_ref, kb_ref, vb_ref, z_ref, xbc_ref,
                 gate_ref, dt_ref):
    u = _rms(h_ref[...], g_ref[...]).astype(BF16)
    step = 512

    def cols(c0, n):
        for c in range(0, n, step):
            w = min(step, n - c)
            yield c, w, _dot(u, w_ref[:, c0 + c:c0 + c + w])

    for c, w, r in cols(C_Q, ATTN_W):
        q_ref[:, c:c + w] = r.astype(BF16)
    for c, w, r in cols(C_K, ATTN_W):
        k_ref[:, c:c + w] = r
        kb_ref[:, c:c + w] = r.astype(BF16)
    for c, w, r in cols(C_V, ATTN_W):
        v_ref[:, c:c + w] = r
        vb_ref[:, c:c + w] = r.astype(BF16)
    for c, w, r in cols(C_Z, D_INNER):
        z_ref[:, c:c + w] = r
    for c, w, r in cols(C_XBC, CONV_DIM):
        xbc_ref[:, c:c + w] = r
    for c, w, r in cols(C_GATE, 2 * D_MODEL):
        gate_ref[:, c:c + w] = r
    for c, w, r in cols(C_DT, LANES):
        dt_ref[:, c:c + w] = r


def _proj(h, gain, w):
    t = h.shape[0]
    tm = min(256, t)
    assert t % tm == 0
    row = lambda n: pl.BlockSpec((tm, n), lambda i: (i, 0))
    out_shape = (
        jax.ShapeDtypeStruct((t, ATTN_W), BF16),
        jax.ShapeDtypeStruct((t, ATTN_W), F32),
        jax.ShapeDtypeStruct((t, ATTN_W), F32),
        jax.ShapeDtypeStruct((t, ATTN_W), BF16),
        jax.ShapeDtypeStruct((t, ATTN_W), BF16),
        jax.ShapeDtypeStruct((t, D_INNER), F32),
        jax.ShapeDtypeStruct((t, CONV_DIM), F32),
        jax.ShapeDtypeStruct((t, 2 * D_MODEL), F32),
        jax.ShapeDtypeStruct((t, LANES), F32),
    )
    return pl.pallas_call(
        _proj_kernel,
        grid=(t // tm,),
        in_specs=[row(D_MODEL), _resident((1, D_MODEL)), _resident((D_MODEL, IN_COLS_PAD))],
        out_specs=tuple(row(s.shape[1]) for s in out_shape),
        out_shape=out_shape,
        compiler_params=_cparams(("parallel",)),
        name="proj",
    )(h, gain, w)


def _attn_prompt_kernel(rb_ref, lq1, lk1, lq2, lk2, subln_ref, q_ref, k_ref, v_ref, o_ref,
                        qs_ref, m_ref, l_ref, acc_ref, bias_ref, *, tq, tk, lam_init, far_unroll):
    h = pl.program_id(1)
    i = pl.program_id(2)

    @pl.when(i == 0)
    def _build_bias():
        r = lax.broadcasted_iota(jnp.int32, (tq, tk), 0)
        c = lax.broadcasted_iota(jnp.int32, (tq, tk), 1)
        for which in range(2):
            dist = r - c + which * tk
            b = _t5_bias(dist, lambda j: rb_ref[j, h])
            if which == 0:
                b = jnp.where(dist >= 0, b, NEG_BIG)
            bias_ref[which] = b

    q = q_ref[...]
    lane = lax.broadcasted_iota(jnp.int32, q.shape, 1)
    zero = jnp.zeros_like(q)
    qs_ref[0:tq, :] = jnp.where(lane < HD_A, q, zero)
    qs_ref[tq:2 * tq, :] = jnp.where(lane >= HD_A, q, zero)
    m_ref[...] = jnp.full(m_ref.shape, NEG_BIG, F32)
    l_ref[...] = jnp.zeros(l_ref.shape, F32)
    acc_ref[...] = jnp.zeros(acc_ref.shape, F32)

    def update(tiles):
        qs = qs_ref[...]
        starts = [pl.multiple_of(j * tk, tk) for j, _ in tiles]
        scores = [_dot_nt(qs, k_ref[pl.ds(st, tk), :]) for st in starts]
        for s, st, (_, bias) in zip(scores, starts, tiles):
            if bias is not None:
                s = (s.reshape(2, tq, tk) + bias[None]).reshape(2 * tq, tk)
            m_prev = m_ref[...]
            m_new = jnp.maximum(m_prev, jnp.max(s, axis=1, keepdims=True))
            alpha = jnp.exp(m_prev - m_new)
            p = jnp.exp(s - jnp.tile(m_new, (1, tk // LANES)))
            l_ref[...] = alpha * l_ref[...] + jnp.sum(p, axis=1, keepdims=True)
            m_ref[...] = m_new
            acc_ref[...] = alpha * acc_ref[...] + _dot(p.astype(BF16), v_ref[pl.ds(st, tk), :])

    @pl.when(i == 0)
    def _diag_only():
        update([(i, bias_ref[0])])

    @pl.when(i >= 1)
    def _diag_and_prev():
        update([(i, bias_ref[0]), (i - 1, bias_ref[1])])

    n_far = jnp.maximum(i - 1, 0)
    n_quads = n_far // far_unroll

    def far_body(g, carry):
        update([(g * far_unroll + u, None) for u in range(far_unroll)])
        return carry

    lax.fori_loop(0, n_quads, far_body, 0)
    base = n_quads * far_unroll
    rem = n_far - base
    width = far_unroll // 2
    while width >= 1:
        take = (rem & width) != 0

        @pl.when(take)
        def _remainder(base=base, width=width):
            update([(base + u, None) for u in range(width)])

        base = base + jnp.where(take, width, 0)
        width //= 2

    lam = _diff_lambda(lq1, lk1, lq2, lk2, lam_init)
    o = acc_ref[...] / l_ref[...]
    o = o[0:tq] - lam * o[tq:2 * tq]
    o = _rms(o, subln_ref[...]) * (1.0 - lam_init)
    o_ref[...] = o.astype(o_ref.dtype)


def _attn_prompt(q, k, v, rel_bias, lams, subln, lam_init):
    b, l, _ = q.shape
    tq = tk = 256
    assert l % tq == 0 and tk >= MAX_DISTANCE
    vec = lambda n: pl.BlockSpec((1, n), lambda b_, h_, i_: (0, 0))
    kern = functools.partial(_attn_prompt_kernel, tq=tq, tk=tk, lam_init=lam_init, far_unroll=4)
    return pl.pallas_call(
        kern,
        grid=(b, N_HEADS_A, l // tq),
        in_specs=[
            pl.BlockSpec(memory_space=pltpu.SMEM),
            vec(HD_A), vec(HD_A), vec(HD_A), vec(HD_A), vec(DV_A),
            pl.BlockSpec((None, tq, DV_A), lambda b_, h_, i_: (b_, i_, h_)),
            pl.BlockSpec((None, l, DV_A), lambda b_, h_, i_: (b_, 0, h_)),
            pl.BlockSpec((None, l, DV_A), lambda b_, h_, i_: (b_, 0, h_)),
        ],
        out_specs=pl.BlockSpec((None, tq, DV_A), lambda b_, h_, i_: (b_, i_, h_)),
        out_shape=jax.ShapeDtypeStruct((b, l, ATTN_W), BF16),
        scratch_shapes=[
            pltpu.VMEM((2 * tq, DV_A), BF16),
            pltpu.VMEM((2 * tq, LANES), F32),
            pltpu.VMEM((2 * tq, LANES), F32),
            pltpu.VMEM((2 * tq, DV_A), F32),
            pltpu.VMEM((2, tq, tk), F32),
        ],
        compiler_params=_cparams(("parallel", "parallel", "arbitrary")),
        name="attn_prompt",
    )(rel_bias, *lams, subln, q, k, v)


PAGES_PER_STEP = 8
PAGES_PER_TILE = 2


def _attn_sample_kernel(pt_ref, rbx_ref, lq1, lk1, lq2, lk2, subln_ref, qt_ref, *rest,
                        page, n_new, lam_init):
    del pt_ref
    n_pp = PAGES_PER_STEP
    kc_refs, vc_refs = rest[:n_pp], rest[n_pp:2 * n_pp]
    kn_ref, vn_ref, o_ref, m_ref, l_ref, acc_ref, bias_ref, biasn_ref = rest[2 * n_pp:]
    b = pl.program_id(0)
    p = pl.program_id(1)
    last = p == pl.num_programs(1) - 1
    nh = N_HEADS_A
    rows = 2 * nh * n_new
    tile_tok = PAGES_PER_TILE * page

    @pl.when((b == 0) & (p == 0))
    def _build_bias():
        rb = lambda j: rbx_ref[:, j:j + 1]
        r = lax.broadcasted_iota(jnp.int32, (rows, tile_tok), 0)
        tok = lax.broadcasted_iota(jnp.int32, (rows, tile_tok), 1)
        bias_ref[...] = _t5_bias(tile_tok + r % n_new - tok, rb)
        r = lax.broadcasted_iota(jnp.int32, (rows, page), 0)
        tok = lax.broadcasted_iota(jnp.int32, (rows, page), 1)
        dist = r % n_new - tok
        biasn_ref[...] = jnp.where(dist >= 0, _t5_bias(dist, rb), NEG_BIG)

    @pl.when(p == 0)
    def _init():
        m_ref[...] = jnp.full(m_ref.shape, NEG_BIG, F32)
        l_ref[...] = jnp.zeros(l_ref.shape, F32)
        acc_ref[...] = jnp.zeros(acc_ref.shape, F32)

    qt = qt_ref[...]

    def flat(refs):
        parts = []
        for ref in refs:
            parts.append(jnp.concatenate(
                [ref[pl.ds(h, page, stride=nh), :].astype(BF16) for h in range(nh)], axis=1))
        return parts[0] if len(parts) == 1 else jnp.concatenate(parts, axis=0)

    def update(tiles):
        scores = [_dot_nt(qt, kf) for kf, _, _ in tiles]
        for st, (_, vf, extra) in zip(scores, tiles):
            n_tok = st.shape[1]
            if extra is not None:
                st = st + extra
            m_prev = m_ref[...]
            m_new = jnp.maximum(m_prev, jnp.max(st, axis=1, keepdims=True))
            alpha = jnp.exp(m_prev - m_new)
            pr = jnp.exp(st - jnp.tile(m_new, (1, n_tok // LANES)))
            l_ref[...] = alpha * l_ref[...] + jnp.sum(pr, axis=1, keepdims=True)
            m_ref[...] = m_new
            o_all = _dot(pr.astype(BF16), vf)
            per_head = rows // nh
            pv = jnp.concatenate(
                [o_all[h * per_head:(h + 1) * per_head, h * DV_A:(h + 1) * DV_A] for h in range(nh)],
                axis=0)
            acc_ref[...] = alpha * acc_ref[...] + pv

    n_tiles = PAGES_PER_STEP // PAGES_PER_TILE
    tiles = []
    for t in range(n_tiles):
        sl = slice(t * PAGES_PER_TILE, (t + 1) * PAGES_PER_TILE)
        extra = None
        if t == n_tiles - 1:
            extra = jnp.where(last, bias_ref[...], 0.0)
        tiles.append((flat(kc_refs[sl]), flat(vc_refs[sl]), extra))
    update(tiles)

    @pl.when(last)
    def _new_tokens_and_finish():
        update([(flat([kn_ref]), flat([vn_ref]), biasn_ref[...])])
        lam = _diff_lambda(lq1, lk1, lq2, lk2, lam_init)
        on = (acc_ref[...] / l_ref[...]).reshape(nh, 2, n_new, DV_A)
        o = on[:, 0] - lam * on[:, 1]
        o_ref[...] = _rms(o, subln_ref[...]) * (1.0 - lam_init)


def _attn_sample(layer, q, k_new, v_new, cache_k, cache_v, page_table, rel_bias, lams, subln,
                 lam_init):
    s, n_new, _ = q.shape
    n_pages = page_table.shape[1]
    page = cache_k.shape[2]
    nh = N_HEADS_A
    rows = 2 * nh * n_new
    assert rows == LANES and page == LANES and n_pages % PAGES_PER_STEP == 0
    assert PAGES_PER_TILE * page - page + 1 >= MAX_DISTANCE
    qh = jnp.transpose(q.reshape(s, n_new, nh, DV_A), (0, 2, 1, 3))
    half = jnp.arange(2)[:, None, None] == (jnp.arange(DV_A) // HD_A)[None, None, :]
    zero = jnp.zeros((), q.dtype)
    qm = jnp.where(half[None, None], qh[:, :, None], zero)
    eye = jnp.eye(nh, dtype=bool)
    qt = jnp.where(eye[None, :, None, None, :, None], qm[:, :, :, :, None, :], zero)
    qt = qt.reshape(s, rows, nh * DV_A)
    rbx = jnp.repeat(rel_bias.T, 2 * n_new, axis=0)
    pad = ((0, 0), (0, page - n_new), (0, 0), (0, 0))
    k_pad = jnp.pad(k_new, pad).reshape(s, page * nh, DV_A)
    v_pad = jnp.pad(v_new, pad).reshape(s, page * nh, DV_A)
    cache_k = cache_k.reshape(cache_k.shape[:2] + (page * nh, DV_A))
    cache_v = cache_v.reshape(cache_v.shape[:2] + (page * nh, DV_A))
    vec = lambda n: pl.BlockSpec((1, n), lambda b_, p_, pt: (0, 0))

    def page_spec(u):
        return pl.BlockSpec((None, None, page * nh, DV_A),
                            lambda b_, p_, pt: (layer, pt[b_, p_ * PAGES_PER_STEP + u], 0, 0))

    new_spec = pl.BlockSpec((None, page * nh, DV_A), lambda b_, p_, pt: (b_, 0, 0))
    kern = functools.partial(_attn_sample_kernel, page=page, n_new=n_new, lam_init=lam_init)
    grid_spec = pltpu.PrefetchScalarGridSpec(
        num_scalar_prefetch=1,
        grid=(s, n_pages // PAGES_PER_STEP),
        in_specs=[
            pl.BlockSpec((rows, N_BUCKETS), lambda b_, p_, pt: (0, 0)),
            vec(HD_A), vec(HD_A), vec(HD_A), vec(HD_A), vec(DV_A),
            pl.BlockSpec((None, rows, nh * DV_A), lambda b_, p_, pt: (b_, 0, 0)),
            *[page_spec(u) for u in range(PAGES_PER_STEP)],
            *[page_spec(u) for u in range(PAGES_PER_STEP)],
            new_spec, new_spec,
        ],
        out_specs=pl.BlockSpec((None, nh, n_new, DV_A), lambda b_, p_, pt: (b_, 0, 0, 0)),
        scratch_shapes=[
            pltpu.VMEM((rows, LANES), F32),
            pltpu.VMEM((rows, LANES), F32),
            pltpu.VMEM((rows, DV_A), F32),
            pltpu.VMEM((rows, PAGES_PER_TILE * page), F32),
            pltpu.VMEM((rows, page), F32),
        ],
    )
    o = pl.pallas_call(
        kern,
        grid_spec=grid_spec,
        out_shape=jax.ShapeDtypeStruct((s, nh, n_new, DV_A), F32),
        compiler_params=_cparams(("arbitrary", "arbitrary")),
        name="attn_sample",
    )(page_table, rbx, *lams, subln, qt, *([cache_k] * PAGES_PER_STEP),
      *([cache_v] * PAGES_PER_STEP), k_pad, v_pad)
    return jnp.transpose(o, (0, 2, 1, 3)).reshape(s * n_new, ATTN_W).astype(BF16)


def _ssd_prepare(xp_rows, dtr, cw_ref, cb_ref, dtb_ref, alog_ref, dsk_ref, seq_len,
                 xs_s, bm_s, cm_s, dt_s, acs_s, aend_s, acst_s, dsk_s):
    r = SSD_ROWS
    y = cb_ref[...] + xp_rows(0) * cw_ref[0:1, :]
    for j in range(1, CONV_W):
        y = y + xp_rows(j) * cw_ref[j:j + 1, :]
    xc = _silu(y)
    dt = jax.nn.softplus(dtr + dtb_ref[...])
    a = -jnp.exp(alog_ref[...])
    dta = dt * a
    row = lax.broadcasted_iota(jnp.int32, (r, r), 0)
    col = lax.broadcasted_iota(jnp.int32, (r, r), 1)
    same = (row // seq_len) == (col // seq_len)
    tri = (same & (col <= row)).astype(F32)
    a_cs = jnp.dot(tri, dta, precision=lax.Precision.HIGHEST, preferred_element_type=F32)
    a_end = jnp.dot(same.astype(F32), dta, precision=lax.Precision.HIGHEST,
                    preferred_element_type=F32)
    a_cs_t = a_cs.T
    dsk = jnp.broadcast_to(dsk_ref[...], (8, LANES))
    for g in range(N_GROUPS_B):
        xs_s[g] = xc[:, g * GROUP_W:(g + 1) * GROUP_W]
        bm_s[g] = xc[:, D_INNER + g * D_STATE:D_INNER + (g + 1) * D_STATE].astype(BF16)
        c0 = D_INNER + N_GROUPS_B * D_STATE + g * D_STATE
        cm_s[g] = xc[:, c0:c0 + D_STATE].astype(BF16)
        shift = (LANES - HEADS_PER_GROUP * g) % LANES
        roll = (lambda x: x) if shift == 0 else (lambda x: pltpu.roll(x, shift, 1))
        dt_s[g] = roll(dt)
        acs_s[g] = roll(a_cs)
        aend_s[g] = roll(a_end)
        dsk_s[g] = roll(dsk)
        acst_s[g] = a_cs_t[g * HEADS_PER_GROUP:(g + 1) * HEADS_PER_GROUP, :]


def _ssd_group(g, seq_len, z_g, nw_g, state_t, xs_s, bm_s, cm_s, dt_s, acs_s, aend_s, acst_s,
               dsk_s):
    r = SSD_ROWS
    n_seq = r // seq_len
    xs_g = xs_s[g]
    bmb = bm_s[g]
    cmb = cm_s[g]
    dtg = dt_s[g]
    acs = acs_s[g]
    aend = aend_s[g]
    acst = acst_s[g]
    dsk = dsk_s[g]
    exp_acs = jnp.exp(acs)
    to_end = jnp.exp(aend - acs)
    cdec = jnp.exp(aend)

    row = lax.broadcasted_iota(jnp.int32, (r, r), 0)
    col = lax.broadcasted_iota(jnp.int32, (r, r), 1)
    tri = ((row // seq_len) == (col // seq_len)) & (col <= row)
    row_seq = lax.broadcasted_iota(jnp.int32, (r, 1), 0) // seq_len

    cb = _dot_nt(cmb, bmb)

    if n_seq == 1:
        y_off = _dot(cmb, state_t[0].astype(BF16))
    else:
        y_off = jnp.zeros((r, GROUP_W), F32)
        for j in range(n_seq):
            y_off = y_off + jnp.where(row_seq == j, _dot(cmb, state_t[j].astype(BF16)), 0.0)

    left = lax.broadcasted_iota(jnp.int32, (1, LANES), 1) < HEADDIM_B
    gz_parts, xw_parts, dec_parts = [], [], []

    for pp in range(HEADS_PER_GROUP // 2):
        h0, h1 = 2 * pp, 2 * pp + 1
        lo, hi = pp * LANES, (pp + 1) * LANES

        def expand2(arr):
            return jnp.where(left, arr[:, h0:h0 + 1], arr[:, h1:h1 + 1])

        xs_p = xs_g[:, lo:hi]
        xdt = xs_p * expand2(dtg)
        xdt_b = xdt.astype(BF16)
        zero_b = jnp.zeros_like(xdt_b)
        y = y_off[:, lo:hi] * expand2(exp_acs) + expand2(dsk[0:1, :]) * xs_p
        for hh, keep in ((h0, left), (h1, jnp.logical_not(left))):
            seg = acs[:, hh:hh + 1] - acst[hh:hh + 1, :]
            decay = jnp.exp(jnp.where(tri, seg, -jnp.inf))
            gm = (cb * decay).astype(BF16)
            y = y + _dot(gm, jnp.where(keep, xdt_b, zero_b))
        gz_parts.append(y * _silu(z_g[:, lo:hi]))
        xw_parts.append((xdt * expand2(to_end)).astype(BF16))
        dec_parts.append(expand2(cdec))

    gz = jnp.concatenate(gz_parts, axis=1)
    ms = jnp.mean(gz * gz, axis=-1, keepdims=True)
    out = gz * lax.rsqrt(ms + EPS) * nw_g

    xw = jnp.concatenate(xw_parts, axis=1)
    dec = jnp.concatenate(dec_parts, axis=1)
    new_states = []
    for j in range(n_seq):
        if n_seq == 1:
            bsel = bmb
        else:
            bsel = jnp.where(row_seq == j, bmb, jnp.zeros_like(bmb))
        new_states.append(state_t[j] * dec[j * seq_len:j * seq_len + 1, :] + _dot_tn(bsel, xw))
    return out, new_states


_SSD_SCRATCH = [
    pltpu.VMEM((N_GROUPS_B, SSD_ROWS, GROUP_W), F32),
    pltpu.VMEM((N_GROUPS_B, SSD_ROWS, D_STATE), BF16),
    pltpu.VMEM((N_GROUPS_B, SSD_ROWS, D_STATE), BF16),
    pltpu.VMEM((N_GROUPS_B, SSD_ROWS, LANES), F32),
    pltpu.VMEM((N_GROUPS_B, SSD_ROWS, LANES), F32),
    pltpu.VMEM((N_GROUPS_B, SSD_ROWS, LANES), F32),
    pltpu.VMEM((N_GROUPS_B, HEADS_PER_GROUP, SSD_ROWS), F32),
    pltpu.VMEM((N_GROUPS_B, 8, LANES), F32),
]


def _ssd_prompt_kernel(cur_ref, hist_ref, dtr_ref, z_ref, cw_ref, cb_ref, dtb_ref, alog_ref,
                       dsk_ref, nw_ref, o_ref, st_ref, xp_s, ht_s, *scr):
    c = pl.program_id(1)
    n_c = pl.num_programs(1)
    r = SSD_ROWS

    @pl.when(c == 0)
    def _start():
        ht_s[...] = jnp.zeros(ht_s.shape, F32)
        xp_s[0:8, :] = jnp.zeros((8, CONV_DIM), F32)

    @pl.when(c > 0)
    def _history():
        xp_s[0:8, :] = hist_ref[...]

    xp_s[8:8 + r, :] = cur_ref[...]
    xp_rows = lambda j: xp_s[pl.ds(8 - (CONV_W - 1) + j, r), :]
    _ssd_prepare(xp_rows, dtr_ref[...], cw_ref, cb_ref, dtb_ref, alog_ref, dsk_ref, r, *scr[:8])
    for g in range(N_GROUPS_B):
        out, new = _ssd_group(g, r, z_ref[:, g * GROUP_W:(g + 1) * GROUP_W], nw_ref[g],
                              [ht_s[g]], *scr)
        o_ref[:, g * GROUP_W:(g + 1) * GROUP_W] = out.astype(o_ref.dtype)
        ht_s[g] = new[0]

    @pl.when(c == n_c - 1)
    def _emit_state():
        for g in range(N_GROUPS_B):
            st_ref[g * GROUP_W:(g + 1) * GROUP_W, :] = ht_s[g].T


def _ssd_prompt(xbc, dtr, z, cw, cb, dtb, alog, dsk, nw):
    b, l, _ = xbc.shape
    r = SSD_ROWS
    assert l % r == 0
    tile = lambda n: pl.BlockSpec((None, r, n), lambda b_, c_: (b_, c_, 0))
    full = lambda shape: pl.BlockSpec(shape, lambda b_, c_: (0,) * len(shape))
    return pl.pallas_call(
        _ssd_prompt_kernel,
        grid=(b, l // r),
        in_specs=[
            tile(CONV_DIM),
            pl.BlockSpec((None, 8, CONV_DIM), lambda b_, c_: (b_, jnp.maximum(c_ * (r // 8) - 1, 0), 0)),
            tile(LANES), tile(D_INNER),
            full((CONV_W, CONV_DIM)), full((1, CONV_DIM)), full((1, LANES)), full((1, LANES)),
            full((1, LANES)), full((N_GROUPS_B, 1, GROUP_W)),
        ],
        out_specs=(tile(D_INNER), pl.BlockSpec((None, D_INNER, D_STATE), lambda b_, c_: (b_, 0, 0))),
        out_shape=(jax.ShapeDtypeStruct((b, l, D_INNER), BF16),
                   jax.ShapeDtypeStruct((b, D_INNER, D_STATE), F32)),
        scratch_shapes=[pltpu.VMEM((8 + r, CONV_DIM), F32),
                        pltpu.VMEM((N_GROUPS_B, D_STATE, GROUP_W), F32)] + _SSD_SCRATCH,
        compiler_params=_cparams(("parallel", "arbitrary")),
        name="ssd_prompt",
    )(xbc, xbc, dtr, z, cw, cb, dtb, alog, dsk, nw)


def _ssd_sample_kernel(xp_ref, dtr_ref, z_ref, cw_ref, cb_ref, dtb_ref, alog_ref, dsk_ref, nw_ref,
                       h0_ref, o_ref, st_ref, *scr, seq_len):
    g = pl.program_id(1)
    r = SSD_ROWS
    n_seq = r // seq_len

    @pl.when(g == 0)
    def _prepare():
        xp_rows = lambda j: xp_ref[:, pl.ds(j, seq_len), :].reshape(r, CONV_DIM)
        _ssd_prepare(xp_rows, dtr_ref[...], cw_ref, cb_ref, dtb_ref, alog_ref, dsk_ref, seq_len,
                     *scr[:8])

    states = [h0_ref[j].reshape(GROUP_W, D_STATE).T for j in range(n_seq)]
    out, new = _ssd_group(g, seq_len, z_ref[...], nw_ref[...], states, *scr)
    o_ref[...] = out.astype(o_ref.dtype)
    for j in range(n_seq):
        st_ref[j] = new[j].T.reshape(HEADS_PER_GROUP, HEADDIM_B, D_STATE)


def _ssd_sample(layer, xp, dtr, z, state_ssm, cw, cb, dtb, alog, dsk, nw):
    s, win, _ = xp.shape
    seq_len = win - (CONV_W - 1)
    r = SSD_ROWS
    n_seq = r // seq_len
    assert seq_len == 8 and s % n_seq == 0
    full = lambda shape: pl.BlockSpec(shape, lambda t_, g_: (0,) * len(shape))
    kern = functools.partial(_ssd_sample_kernel, seq_len=seq_len)
    return pl.pallas_call(
        kern,
        grid=(s // n_seq, N_GROUPS_B),
        in_specs=[
            pl.BlockSpec((n_seq, win, CONV_DIM), lambda t_, g_: (t_, 0, 0)),
            pl.BlockSpec((r, LANES), lambda t_, g_: (t_, 0)),
            pl.BlockSpec((r, GROUP_W), lambda t_, g_: (t_, g_)),
            full((CONV_W, CONV_DIM)), full((1, CONV_DIM)), full((1, LANES)), full((1, LANES)),
            full((1, LANES)),
            pl.BlockSpec((None, 1, GROUP_W), lambda t_, g_: (g_, 0, 0)),
            pl.BlockSpec((None, n_seq, HEADS_PER_GROUP, HEADDIM_B, D_STATE),
                         lambda t_, g_: (layer, t_, g_, 0, 0)),
        ],
        out_specs=(pl.BlockSpec((r, GROUP_W), lambda t_, g_: (t_, g_)),
                   pl.BlockSpec((n_seq, HEADS_PER_GROUP, HEADDIM_B, D_STATE),
                                lambda t_, g_: (t_, g_, 0, 0))),
        out_shape=(jax.ShapeDtypeStruct((s * seq_len, D_INNER), BF16),
                   jax.ShapeDtypeStruct((s, N_HEADS_B, HEADDIM_B, D_STATE), F32)),
        scratch_shapes=_SSD_SCRATCH,
        compiler_params=_cparams(("parallel", "arbitrary")),
        name="ssd_sample",
    )(xp, dtr, z, cw, cb, dtb, alog, dsk, nw, state_ssm)


def _post_kernel(h_ref, a_ref, s_ref, gate_ref, wa_ref, wb_ref, wo_ref, n1_ref, n2_ref, wg_ref,
                 wu_ref, wd_ref, n3_ref, o_ref):
    ya = _dot(a_ref[...], wa_ref[...])
    yb = _dot(s_ref[...], wb_ref[...])
    merged = (jax.nn.sigmoid(gate_ref[:, 0:D_MODEL]) * ya
              + jax.nn.sigmoid(gate_ref[:, D_MODEL:2 * D_MODEL]) * yb)
    mix = _dot(merged.astype(BF16), wo_ref[...])
    h1 = h_ref[...] + _rms(mix, n1_ref[...])
    u = _rms(h1, n2_ref[...]).astype(BF16)
    ff = jnp.zeros(h1.shape, F32)
    step = 256
    for c in range(0, D_FF, step):
        hid = _silu(_dot(u, wg_ref[:, c:c + step])) * _dot(u, wu_ref[:, c:c + step])
        ff = ff + _dot(hid.astype(BF16), wd_ref[c:c + step, :])
    o_ref[...] = h1 + _rms(ff, n3_ref[...])


def _post(h, attn_o, ssm_o, gates, wa, wb, wo, n1, n2, wg, wu, wd, n3):
    t = h.shape[0]
    tm = min(256, t)
    assert t % tm == 0
    row = lambda n: pl.BlockSpec((tm, n), lambda i: (i, 0))
    return pl.pallas_call(
        _post_kernel,
        grid=(t // tm,),
        in_specs=[row(D_MODEL), row(ATTN_W), row(D_INNER), row(2 * D_MODEL),
                  _resident(wa.shape), _resident(wb.shape), _resident(wo.shape),
                  _resident((1, D_MODEL)), _resident((1, D_MODEL)),
                  _resident(wg.shape), _resident(wu.shape), _resident(wd.shape),
                  _resident((1, D_MODEL))],
        out_specs=row(D_MODEL),
        out_shape=jax.ShapeDtypeStruct((t, D_MODEL), F32),
        compiler_params=_cparams(("parallel",)),
        name="post",
    )(h, attn_o, ssm_o, gates, wa, wb, wo, n1, n2, wg, wu, wd, n3)


def _pad_lanes(x):
    return jnp.pad(x.reshape(1, -1), ((0, 0), (0, LANES - x.shape[-1])))


def kernel(x_prompt, x_sample, cache_k, cache_v, state_ssm, state_conv, page_table, rel_bias,
           norm_mix_pre, w_in, lambda_q1, lambda_k1, lambda_q2, lambda_k2, attn_subln, conv_w,
           conv_b, dt_bias, a_log, d_skip, ssm_norm, w_br_a, w_br_b, w_out, norm_mix_post,
           norm_ffn_pre, w_ffn_gate, w_ffn_up, w_ffn_down, norm_ffn_post):
    bsz, seq, _ = x_prompt.shape
    n_smp, n_new, _ = x_sample.shape
    depth = w_in.shape[0]
    hp = x_prompt.reshape(bsz * seq, D_MODEL)
    hs = x_sample.reshape(n_smp * n_new, D_MODEL)
    row = lambda x: x.reshape(1, -1)
    outs = [[] for _ in range(8)]
    for l in range(depth):
        lam_init = 0.8 - 0.6 * math.exp(-0.3 * l)
        wl = w_in[l]
        w_proj = jnp.concatenate([
            wl[:, 0:ATTN_W] * ATTN_SCALE,
            wl[:, ATTN_W:8192], wl[:, 8224:10272], wl[:, 8192:8224],
            jnp.zeros((D_MODEL, LANES - N_HEADS_B), F32)], axis=1).astype(BF16)
        lams = (row(lambda_q1[l]), row(lambda_k1[l]), row(lambda_q2[l]), row(lambda_k2[l]))
        ssd_w = (conv_w[l], row(conv_b[l]), _pad_lanes(dt_bias[l]), _pad_lanes(a_log[l]),
                 _pad_lanes(d_skip[l]), ssm_norm[l].reshape(N_GROUPS_B, 1, GROUP_W))
        post_w = (w_br_a[l].astype(BF16), w_br_b[l].astype(BF16), w_out[l].astype(BF16),
                  row(norm_mix_post[l]), row(norm_ffn_pre[l]), w_ffn_gate[l].astype(BF16),
                  w_ffn_up[l].astype(BF16), w_ffn_down[l].astype(BF16), row(norm_ffn_post[l]))
        subln = row(attn_subln[l])

        q, k, v, kb, vb, z, xbc, gates, dtr = _proj(hp, row(norm_mix_pre[l]), w_proj)
        b3 = lambda x: x.reshape(bsz, seq, x.shape[-1])
        attn_o = _attn_prompt(b3(q), b3(kb), b3(vb), rel_bias, lams, subln, lam_init)
        ssm_o, st = _ssd_prompt(b3(xbc), b3(dtr), b3(z), *ssd_w)
        hp = _post(hp, attn_o.reshape(bsz * seq, ATTN_W), ssm_o.reshape(bsz * seq, D_INNER), gates,
                   *post_w)
        outs[0].append(k.reshape(bsz, seq, N_HEADS_A, DV_A))
        outs[1].append(v.reshape(bsz, seq, N_HEADS_A, DV_A))
        outs[2].append(st.reshape(bsz, N_HEADS_B, HEADDIM_B, D_STATE))
        outs[3].append(b3(xbc)[:, seq - (CONV_W - 1):, :])

        q, k, v, kb, vb, z, xbc, gates, dtr = _proj(hs, row(norm_mix_pre[l]), w_proj)
        k4 = k.reshape(n_smp, n_new, N_HEADS_A, DV_A)
        v4 = v.reshape(n_smp, n_new, N_HEADS_A, DV_A)
        attn_o = _attn_sample(l, q.reshape(n_smp, n_new, ATTN_W), k4, v4, cache_k, cache_v,
                              page_table, rel_bias, lams, subln, lam_init)
        xp = jnp.concatenate([state_conv[l], xbc.reshape(n_smp, n_new, CONV_DIM)], axis=1)
        ssm_o, st = _ssd_sample(l, xp, dtr, z, state_ssm, *ssd_w)
        hs = _post(hs, attn_o, ssm_o, gates, *post_w)
        outs[4].append(k4)
        outs[5].append(v4)
        outs[6].append(st)
        outs[7].append(xp[:, n_new:, :])

    stk = [jnp.stack(o) for o in outs]
    return (hp.reshape(bsz, seq, D_MODEL), hs.reshape(n_smp, n_new, D_MODEL),
            stk[0], stk[1], stk[2], stk[3], stk[4], stk[5], stk[6], stk[7])
```

```python
import functools
import math

import jax
import jax.numpy as jnp
from jax import lax
from jax.experimental import pallas as pl
from jax.experimental.pallas import tpu as pltpu

F32 = jnp.float32
BF16 = jnp.bfloat16

D_MODEL = 1024
N_HEADS_A = 8
HD_A = 64
DV_A = 128
ATTN_W = N_HEADS_A * DV_A
ATTN_SCALE = HD_A ** -0.5
LOG2E = math.log2(math.e)
N_BUCKETS = 32
MAX_EXACT = 16
MAX_DISTANCE = 128
D_INNER = 2048
HEADDIM_B = 64
N_HEADS_B = 32
N_GROUPS_B = 4
HEADS_PER_GROUP = N_HEADS_B // N_GROUPS_B
GROUP_W = D_INNER // N_GROUPS_B
D_STATE = 128
CONV_W = 4
CONV_DIM = D_INNER + 2 * N_GROUPS_B * D_STATE
SSD_ROWS = 128
D_FF = 2816
EPS = 1e-6
LANES = 128
NEG_BIG = -1e30
VMEM_LIMIT = 56 * 1024 * 1024

C_Q, C_K, C_V = 0, 1024, 2048
C_Z = 3072
C_XBC = C_Z + D_INNER
C_GATE = C_XBC + CONV_DIM
C_DT = C_GATE + 2 * D_MODEL
IN_COLS_PAD = C_DT + LANES


def _cparams(sem):
    return pltpu.CompilerParams(dimension_semantics=sem, vmem_limit_bytes=VMEM_LIMIT)


def _resident(shape):
    nd = len(shape)
    return pl.BlockSpec(shape, lambda *_: (0,) * nd, pipeline_mode=pl.Buffered(1))


def _rms(x, g):
    ms = jnp.mean(x * x, axis=-1, keepdims=True)
    return x * lax.rsqrt(ms + EPS) * g


def _silu(x):
    return x * jax.nn.sigmoid(x)


def _dot(a, b):
    return jnp.dot(a, b, preferred_element_type=F32)


def _dot_nt(a, b):
    return lax.dot_general(a, b, (((1,), (1,)), ((), ())), preferred_element_type=F32)


def _dot_tn(a, b):
    return lax.dot_general(a, b, (((0,), (0,)), ((), ())), preferred_element_type=F32)


def _t5_bias(dist, rb_of_bucket):
    dist = jnp.maximum(dist, 0)
    d = jnp.maximum(dist, MAX_EXACT).astype(F32)
    large = MAX_EXACT + (jnp.log(d / MAX_EXACT) / math.log(MAX_DISTANCE / MAX_EXACT)
                         * (N_BUCKETS - MAX_EXACT)).astype(jnp.int32)
    large = jnp.minimum(large, N_BUCKETS - 1)
    bucket = jnp.where(dist < MAX_EXACT, dist, large)
    far = rb_of_bucket(N_BUCKETS - 1)
    out = jnp.zeros(dist.shape, F32)
    for j in range(N_BUCKETS - 1):
        out = jnp.where(bucket == j, rb_of_bucket(j) - far, out)
    return out * LOG2E


def _diff_lambda(lq1, lk1, lq2, lk2, lam_init):
    s1 = jnp.sum(lq1[...] * lk1[...], axis=-1, keepdims=True)
    s2 = jnp.sum(lq2[...] * lk2[...], axis=-1, keepdims=True)
    return jnp.exp(s1) - jnp.exp(s2) + lam_init


def _proj_kernel(h_ref, g_ref, w_ref, q_ref, k_ref, v_ref, kb_ref, vb_ref, z_ref, xbc_ref,
                 gate_ref, dt_ref):
    u = _rms(h_ref[...], g_ref[...]).astype(BF16)
    step = 512

    def cols(c0, n):
        for c in range(0, n, step):
            w = min(step, n - c)
            yield c, w, _dot(u, w_ref[:, c0 + c:c0 + c + w])

    for c, w, r in cols(C_Q, ATTN_W):
        q_ref[:, c:c + w] = r.astype(BF16)
    for c, w, r in cols(C_K, ATTN_W):
        k_ref[:, c:c + w] = r
        kb_ref[:, c:c + w] = r.astype(BF16)
    for c, w, r in cols(C_V, ATTN_W):
        v_ref[:, c:c + w] = r
        vb_ref[:, c:c + w] = r.astype(BF16)
    for c, w, r in cols(C_Z, D_INNER):
        z_ref[:, c:c + w] = r
    for c, w, r in cols(C_XBC, CONV_DIM):
        xbc_ref[:, c:c + w] = r
    for c, w, r in cols(C_GATE, 2 * D_MODEL):
        gate_ref[:, c:c + w] = r
    for c, w, r in cols(C_DT, LANES):
        dt_ref[:, c:c + w] = r


def _proj(h, gain, w):
    t = h.shape[0]
    tm = min(256, t)
    assert t % tm == 0
    row = lambda n: pl.BlockSpec((tm, n), lambda i: (i, 0))
    out_shape = (
        jax.ShapeDtypeStruct((t, ATTN_W), BF16),
        jax.ShapeDtypeStruct((t, ATTN_W), F32),
        jax.ShapeDtypeStruct((t, ATTN_W), F32),
        jax.ShapeDtypeStruct((t, ATTN_W), BF16),
        jax.ShapeDtypeStruct((t, ATTN_W), BF16),
        jax.ShapeDtypeStruct((t, D_INNER), F32),
        jax.ShapeDtypeStruct((t, CONV_DIM), F32),
        jax.ShapeDtypeStruct((t, 2 * D_MODEL), F32),
        jax.ShapeDtypeStruct((t, LANES), F32),
    )
    return pl.pallas_call(
        _proj_kernel,
        grid=(t // tm,),
        in_specs=[row(D_MODEL), _resident((1, D_MODEL)), _resident((D_MODEL, IN_COLS_PAD))],
        out_specs=tuple(row(s.shape[1]) for s in out_shape),
        out_shape=out_shape,
        compiler_params=_cparams(("parallel",)),
        name="proj",
    )(h, gain, w)


def _attn_prompt_kernel(rb_ref, lq1, lk1, lq2, lk2, subln_ref, q_ref, k_ref, v_ref, o_ref,
                        qst_ref, vt_ref, m_ref, l_ref, acc_ref, bias_ref, *, tq, tk, lam_init,
                        far_unroll):
    h = pl.program_id(1)
    i = pl.program_id(2)
    n_kt = k_ref.shape[0] // tk

    @pl.when(i == 0)
    def _per_head_setup():
        r = lax.broadcasted_iota(jnp.int32, (tk, tq), 0)
        c = lax.broadcasted_iota(jnp.int32, (tk, tq), 1)
        for which in range(2):
            dist = c - r + which * tk
            b = _t5_bias(dist, lambda j: rb_ref[j, h])
            if which == 0:
                b = jnp.where(dist >= 0, b, NEG_BIG)
            bias_ref[which] = jnp.concatenate([b, b], axis=1)

        def transpose_v(j, carry):
            st = pl.multiple_of(j * tk, tk)
            vt_ref[j] = v_ref[pl.ds(st, tk), :].astype(F32).T.astype(BF16)
            return carry

        lax.fori_loop(0, n_kt, transpose_v, 0)

    qt = q_ref[...].astype(F32).T
    d_idx = lax.broadcasted_iota(jnp.int32, qt.shape, 0)
    qst_ref[...] = jnp.concatenate(
        [jnp.where(d_idx < HD_A, qt, 0.0), jnp.where(d_idx >= HD_A, qt, 0.0)], axis=1).astype(BF16)
    m_ref[...] = jnp.full(m_ref.shape, NEG_BIG, F32)
    l_ref[...] = jnp.zeros(l_ref.shape, F32)
    acc_ref[...] = jnp.zeros(acc_ref.shape, F32)

    def update(tiles):
        qst = qst_ref[...]
        scores = [_dot(k_ref[pl.ds(pl.multiple_of(j * tk, tk), tk), :], qst) for j, _ in tiles]
        for s, (j, bias) in zip(scores, tiles):
            if bias is not None:
                s = s + bias
            m_prev = m_ref[...]
            m_new = jnp.maximum(m_prev, jnp.max(s, axis=0, keepdims=True))
            alpha = jnp.exp2(m_prev - m_new)
            p = jnp.exp2(s - m_new[0:1, :])
            l_ref[...] = alpha * l_ref[...] + jnp.sum(p, axis=0, keepdims=True)
            m_ref[...] = m_new
            acc_ref[...] = alpha[0:1, :] * acc_ref[...] + _dot(vt_ref[j], p.astype(BF16))

    @pl.when(i == 0)
    def _diag_only():
        update([(i, bias_ref[0])])

    @pl.when(i >= 1)
    def _diag_and_prev():
        update([(i, bias_ref[0]), (i - 1, bias_ref[1])])

    n_far = jnp.maximum(i - 1, 0)
    n_quads = n_far // far_unroll

    def far_body(g, carry):
        update([(g * far_unroll + u, None) for u in range(far_unroll)])
        return carry

    lax.fori_loop(0, n_quads, far_body, 0)
    base = n_quads * far_unroll
    rem = n_far - base
    width = far_unroll // 2
    while width >= 1:
        take = (rem & width) != 0

        @pl.when(take)
        def _remainder(base=base, width=width):
            update([(base + u, None) for u in range(width)])

        base = base + jnp.where(take, width, 0)
        width //= 2

    lam = _diff_lambda(lq1, lk1, lq2, lk2, lam_init)
    o = acc_ref[...] / l_ref[0:1, :]
    o = (o[:, 0:tq] - lam * o[:, tq:2 * tq]).T
    o = _rms(o, subln_ref[...]) * (1.0 - lam_init)
    o_ref[...] = o.astype(o_ref.dtype)


def _attn_prompt(q, k, v, rel_bias, lams, subln, lam_init):
    b, l, _ = q.shape
    tq = tk = 256
    assert l % tq == 0 and tk >= MAX_DISTANCE
    vec = lambda n: pl.BlockSpec((1, n), lambda b_, h_, i_: (0, 0))
    kern = functools.partial(_attn_prompt_kernel, tq=tq, tk=tk, lam_init=lam_init, far_unroll=8)
    return pl.pallas_call(
        kern,
        grid=(b, N_HEADS_A, l // tq),
        in_specs=[
            pl.BlockSpec(memory_space=pltpu.SMEM),
            vec(HD_A), vec(HD_A), vec(HD_A), vec(HD_A), vec(DV_A),
            pl.BlockSpec((None, tq, DV_A), lambda b_, h_, i_: (b_, i_, h_)),
            pl.BlockSpec((None, l, DV_A), lambda b_, h_, i_: (b_, 0, h_)),
            pl.BlockSpec((None, l, DV_A), lambda b_, h_, i_: (b_, 0, h_)),
        ],
        out_specs=pl.BlockSpec((None, tq, DV_A), lambda b_, h_, i_: (b_, i_, h_)),
        out_shape=jax.ShapeDtypeStruct((b, l, ATTN_W), BF16),
        scratch_shapes=[
            pltpu.VMEM((DV_A, 2 * tq), BF16),
            pltpu.VMEM((l // tk, DV_A, tk), BF16),
            pltpu.VMEM((8, 2 * tq), F32),
            pltpu.VMEM((8, 2 * tq), F32),
            pltpu.VMEM((DV_A, 2 * tq), F32),
            pltpu.VMEM((2, tk, 2 * tq), F32),
        ],
        compiler_params=_cparams(("parallel", "parallel", "arbitrary")),
        name="attn_prompt",
    )(rel_bias, *lams, subln, q, k, v)


PAGES_PER_STEP = 8
PAGES_PER_TILE = 2


def _attn_sample_kernel(pt_ref, rbx_ref, lq1, lk1, lq2, lk2, subln_ref, qt_ref, *rest,
                        page, n_new, lam_init):
    del pt_ref
    n_pp = PAGES_PER_STEP
    kc_refs, vc_refs = rest[:n_pp], rest[n_pp:2 * n_pp]
    kn_ref, vn_ref, o_ref, m_ref, l_ref, acc_ref, bias_ref, biasn_ref = rest[2 * n_pp:]
    b = pl.program_id(0)
    p = pl.program_id(1)
    last = p == pl.num_programs(1) - 1
    nh = N_HEADS_A
    rows = 2 * nh * n_new
    tile_tok = PAGES_PER_TILE * page

    @pl.when((b == 0) & (p == 0))
    def _build_bias():
        rb = lambda j: rbx_ref[:, j:j + 1]
        r = lax.broadcasted_iota(jnp.int32, (rows, tile_tok), 0)
        tok = lax.broadcasted_iota(jnp.int32, (rows, tile_tok), 1)
        bias_ref[...] = _t5_bias(tile_tok + r % n_new - tok, rb)
        r = lax.broadcasted_iota(jnp.int32, (rows, page), 0)
        tok = lax.broadcasted_iota(jnp.int32, (rows, page), 1)
        dist = r % n_new - tok
        biasn_ref[...] = jnp.where(dist >= 0, _t5_bias(dist, rb), NEG_BIG)

    @pl.when(p == 0)
    def _init():
        m_ref[...] = jnp.full(m_ref.shape, NEG_BIG, F32)
        l_ref[...] = jnp.zeros(l_ref.shape, F32)
        acc_ref[...] = jnp.zeros(acc_ref.shape, F32)

    qt = qt_ref[...]

    def flat(refs):
        parts = []
        for ref in refs:
            parts.append(jnp.concatenate(
                [ref[pl.ds(h, page, stride=nh), :].astype(BF16) for h in range(nh)], axis=1))
        return parts[0] if len(parts) == 1 else jnp.concatenate(parts, axis=0)

    def update(tiles):
        scores = [_dot_nt(qt, kf) for kf, _, _ in tiles]
        for st, (_, vf, extra) in zip(scores, tiles):
            n_tok = st.shape[1]
            if extra is not None:
                st = st + extra
            m_prev = m_ref[...]
            m_new = jnp.maximum(m_prev, jnp.max(st, axis=1, keepdims=True))
            alpha = jnp.exp2(m_prev - m_new)
            pr = jnp.exp2(st - jnp.tile(m_new, (1, n_tok // LANES)))
            l_ref[...] = alpha * l_ref[...] + jnp.sum(pr, axis=1, keepdims=True)
            m_ref[...] = m_new
            o_all = _dot(pr.astype(BF16), vf)
            per_head = rows // nh
            pv = jnp.concatenate(
                [o_all[h * per_head:(h + 1) * per_head, h * DV_A:(h + 1) * DV_A] for h in range(nh)],
                axis=0)
            acc_ref[...] = alpha * acc_ref[...] + pv

    n_tiles = PAGES_PER_STEP // PAGES_PER_TILE
    tiles = []
    for t in range(n_tiles):
        sl = slice(t * PAGES_PER_TILE, (t + 1) * PAGES_PER_TILE)
        extra = None
        if t == n_tiles - 1:
            extra = jnp.where(last, bias_ref[...], 0.0)
        tiles.append((flat(kc_refs[sl]), flat(vc_refs[sl]), extra))
    update(tiles)

    @pl.when(last)
    def _new_tokens_and_finish():
        update([(flat([kn_ref]), flat([vn_ref]), biasn_ref[...])])
        lam = _diff_lambda(lq1, lk1, lq2, lk2, lam_init)
        on = (acc_ref[...] / l_ref[...]).reshape(nh, 2, n_new, DV_A)
        o = on[:, 0] - lam * on[:, 1]
        o_ref[...] = _rms(o, subln_ref[...]) * (1.0 - lam_init)


def _attn_sample(layer, q, k_new, v_new, cache_k, cache_v, page_table, rel_bias, lams, subln,
                 lam_init):
    s, n_new, _ = q.shape
    n_pages = page_table.shape[1]
    page = cache_k.shape[2]
    nh = N_HEADS_A
    rows = 2 * nh * n_new
    assert rows == LANES and page == LANES and n_pages % PAGES_PER_STEP == 0
    assert PAGES_PER_TILE * page - page + 1 >= MAX_DISTANCE
    qh = jnp.transpose(q.reshape(s, n_new, nh, DV_A), (0, 2, 1, 3))
    half = jnp.arange(2)[:, None, None] == (jnp.arange(DV_A) // HD_A)[None, None, :]
    zero = jnp.zeros((), q.dtype)
    qm = jnp.where(half[None, None], qh[:, :, None], zero)
    eye = jnp.eye(nh, dtype=bool)
    qt = jnp.where(eye[None, :, None, None, :, None], qm[:, :, :, :, None, :], zero)
    qt = qt.reshape(s, rows, nh * DV_A)
    rbx = jnp.repeat(rel_bias.T, 2 * n_new, axis=0)
    pad = ((0, 0), (0, page - n_new), (0, 0), (0, 0))
    k_pad = jnp.pad(k_new, pad).reshape(s, page * nh, DV_A)
    v_pad = jnp.pad(v_new, pad).reshape(s, page * nh, DV_A)
    cache_k = cache_k.reshape(cache_k.shape[:2] + (page * nh, DV_A))
    cache_v = cache_v.reshape(cache_v.shape[:2] + (page * nh, DV_A))
    vec = lambda n: pl.BlockSpec((1, n), lambda b_, p_, pt: (0, 0))

    def page_spec(u):
        return pl.BlockSpec((None, None, page * nh, DV_A),
                            lambda b_, p_, pt: (layer, pt[b_, p_ * PAGES_PER_STEP + u], 0, 0))

    new_spec = pl.BlockSpec((None, page * nh, DV_A), lambda b_, p_, pt: (b_, 0, 0))
    kern = functools.partial(_attn_sample_kernel, page=page, n_new=n_new, lam_init=lam_init)
    grid_spec = pltpu.PrefetchScalarGridSpec(
        num_scalar_prefetch=1,
        grid=(s, n_pages // PAGES_PER_STEP),
        in_specs=[
            pl.BlockSpec((rows, N_BUCKETS), lambda b_, p_, pt: (0, 0)),
            vec(HD_A), vec(HD_A), vec(HD_A), vec(HD_A), vec(DV_A),
            pl.BlockSpec((None, rows, nh * DV_A), lambda b_, p_, pt: (b_, 0, 0)),
            *[page_spec(u) for u in range(PAGES_PER_STEP)],
            *[page_spec(u) for u in range(PAGES_PER_STEP)],
            new_spec, new_spec,
        ],
        out_specs=pl.BlockSpec((None, nh, n_new, DV_A), lambda b_, p_, pt: (b_, 0, 0, 0)),
        scratch_shapes=[
            pltpu.VMEM((rows, LANES), F32),
            pltpu.VMEM((rows, LANES), F32),
            pltpu.VMEM((rows, DV_A), F32),
            pltpu.VMEM((rows, PAGES_PER_TILE * page), F32),
            pltpu.VMEM((rows, page), F32),
        ],
    )
    o = pl.pallas_call(
        kern,
        grid_spec=grid_spec,
        out_shape=jax.ShapeDtypeStruct((s, nh, n_new, DV_A), F32),
        compiler_params=_cparams(("arbitrary", "arbitrary")),
        name="attn_sample",
    )(page_table, rbx, *lams, subln, qt, *([cache_k] * PAGES_PER_STEP),
      *([cache_v] * PAGES_PER_STEP), k_pad, v_pad)
    return jnp.transpose(o, (0, 2, 1, 3)).reshape(s * n_new, ATTN_W).astype(BF16)


def _ssd_prepare(xp_rows, dtr, cw_ref, cb_ref, dtb_ref, alog_ref, dsk_ref, seq_len,
                 xs_s, bm_s, cm_s, dt_s, acs_s, aend_s, acst_s, dsk_s):
    r = SSD_ROWS
    y = cb_ref[...] + xp_rows(0) * cw_ref[0:1, :]
    for j in range(1, CONV_W):
        y = y + xp_rows(j) * cw_ref[j:j + 1, :]
    xc = _silu(y)
    dt = jax.nn.softplus(dtr + dtb_ref[...])
    a = -jnp.exp(alog_ref[...])
    dta = dt * a
    row = lax.broadcasted_iota(jnp.int32, (r, r), 0)
    col = lax.broadcasted_iota(jnp.int32, (r, r), 1)
    same = (row // seq_len) == (col // seq_len)
    tri = (same & (col <= row)).astype(F32)
    a_cs = jnp.dot(tri, dta, precision=lax.Precision.HIGHEST, preferred_element_type=F32)
    a_end = jnp.dot(same.astype(F32), dta, precision=lax.Precision.HIGHEST,
                    preferred_element_type=F32)
    a_cs_t = a_cs.T
    dsk = jnp.broadcast_to(dsk_ref[...], (8, LANES))
    for g in range(N_GROUPS_B):
        xs_s[g] = xc[:, g * GROUP_W:(g + 1) * GROUP_W]
        bm_s[g] = xc[:, D_INNER + g * D_STATE:D_INNER + (g + 1) * D_STATE].astype(BF16)
        c0 = D_INNER + N_GROUPS_B * D_STATE + g * D_STATE
        cm_s[g] = xc[:, c0:c0 + D_STATE].astype(BF16)
        shift = (LANES - HEADS_PER_GROUP * g) % LANES
        roll = (lambda x: x) if shift == 0 else (lambda x: pltpu.roll(x, shift, 1))
        dt_s[g] = roll(dt)
        acs_s[g] = roll(a_cs)
        aend_s[g] = roll(a_end)
        dsk_s[g] = roll(dsk)
        acst_s[g] = a_cs_t[g * HEADS_PER_GROUP:(g + 1) * HEADS_PER_GROUP, :]


def _ssd_group(g, seq_len, z_g, nw_g, state_t, xs_s, bm_s, cm_s, dt_s, acs_s, aend_s, acst_s,
               dsk_s):
    r = SSD_ROWS
    n_seq = r // seq_len
    xs_g = xs_s[g]
    bmb = bm_s[g]
    cmb = cm_s[g]
    dtg = dt_s[g]
    acs = acs_s[g]
    aend = aend_s[g]
    acst = acst_s[g]
    dsk = dsk_s[g]
    aend = aend[0:1, :] if n_seq == 1 else aend

    row = lax.broadcasted_iota(jnp.int32, (r, r), 0)
    col = lax.broadcasted_iota(jnp.int32, (r, r), 1)
    tri = ((row // seq_len) == (col // seq_len)) & (col <= row)
    row_seq = lax.broadcasted_iota(jnp.int32, (r, 1), 0) // seq_len

    cb = _dot_nt(cmb, bmb)

    if n_seq == 1:
        y_off = _dot(cmb, state_t[0].astype(BF16))
    else:
        y_off = jnp.zeros((r, GROUP_W), F32)
        for j in range(n_seq):
            y_off = y_off + jnp.where(row_seq == j, _dot(cmb, state_t[j].astype(BF16)), 0.0)

    left = lax.broadcasted_iota(jnp.int32, (1, LANES), 1) < HEADDIM_B
    gz_parts, xw_parts, dec_parts = [], [], []

    for pp in range(HEADS_PER_GROUP // 2):
        h0, h1 = 2 * pp, 2 * pp + 1
        lo, hi = pp * LANES, (pp + 1) * LANES

        def expand2(arr):
            return jnp.where(left, arr[:, h0:h0 + 1], arr[:, h1:h1 + 1])

        cols = [jnp.broadcast_to(acs[:, hh:hh + 1], (r, LANES)) for hh in (h0, h1)]
        acs2 = jnp.where(left, cols[0], cols[1])
        aend2 = expand2(aend)
        xs_p = xs_g[:, lo:hi]
        xdt = xs_p * expand2(dtg)
        xdt_b = xdt.astype(BF16)
        zero_b = jnp.zeros_like(xdt_b)
        y = y_off[:, lo:hi] * jnp.exp(acs2) + expand2(dsk[0:1, :]) * xs_p
        for hh, keep, colb in ((h0, left, cols[0]), (h1, jnp.logical_not(left), cols[1])):
            seg = colb - acst[hh:hh + 1, :]
            decay = jnp.exp(jnp.where(tri, seg, -jnp.inf))
            gm = (cb * decay).astype(BF16)
            y = y + _dot(gm, jnp.where(keep, xdt_b, zero_b))
        gz_parts.append(y * _silu(z_g[:, lo:hi]))
        xw_parts.append((xdt * jnp.exp(aend2 - acs2)).astype(BF16))
        dec_parts.append(jnp.exp(aend2))

    gz = jnp.concatenate(gz_parts, axis=1)
    ms = jnp.mean(gz * gz, axis=-1, keepdims=True)
    out = gz * lax.rsqrt(ms + EPS) * nw_g

    xw = jnp.concatenate(xw_parts, axis=1)
    dec = jnp.concatenate(dec_parts, axis=1)
    new_states = []
    for j in range(n_seq):
        if n_seq == 1:
            bsel = bmb
        else:
            bsel = jnp.where(row_seq == j, bmb, jnp.zeros_like(bmb))
        new_states.append(state_t[j] * dec[j * seq_len:j * seq_len + 1, :] + _dot_tn(bsel, xw))
    return out, new_states


_SSD_SCRATCH = [
    pltpu.VMEM((N_GROUPS_B, SSD_ROWS, GROUP_W), F32),
    pltpu.VMEM((N_GROUPS_B, SSD_ROWS, D_STATE), BF16),
    pltpu.VMEM((N_GROUPS_B, SSD_ROWS, D_STATE), BF16),
    pltpu.VMEM((N_GROUPS_B, SSD_ROWS, LANES), F32),
    pltpu.VMEM((N_GROUPS_B, SSD_ROWS, LANES), F32),
    pltpu.VMEM((N_GROUPS_B, SSD_ROWS, LANES), F32),
    pltpu.VMEM((N_GROUPS_B, HEADS_PER_GROUP, SSD_ROWS), F32),
    pltpu.VMEM((N_GROUPS_B, 8, LANES), F32),
]


def _ssd_prompt_kernel(cur_ref, hist_ref, dtr_ref, z_ref, cw_ref, cb_ref, dtb_ref, alog_ref,
                       dsk_ref, nw_ref, o_ref, st_ref, xp_s, ht_s, *scr):
    c = pl.program_id(1)
    n_c = pl.num_programs(1)
    r = SSD_ROWS

    @pl.when(c == 0)
    def _start():
        ht_s[...] = jnp.zeros(ht_s.shape, F32)
        xp_s[0:8, :] = jnp.zeros((8, CONV_DIM), F32)

    @pl.when(c > 0)
    def _history():
        xp_s[0:8, :] = hist_ref[...]

    xp_s[8:8 + r, :] = cur_ref[...]
    xp_rows = lambda j: xp_s[pl.ds(8 - (CONV_W - 1) + j, r), :]
    _ssd_prepare(xp_rows, dtr_ref[...], cw_ref, cb_ref, dtb_ref, alog_ref, dsk_ref, r, *scr[:8])
    for g in range(N_GROUPS_B):
        out, new = _ssd_group(g, r, z_ref[:, g * GROUP_W:(g + 1) * GROUP_W], nw_ref[g],
                              [ht_s[g]], *scr)
        o_ref[:, g * GROUP_W:(g + 1) * GROUP_W] = out.astype(o_ref.dtype)
        ht_s[g] = new[0]

    @pl.when(c == n_c - 1)
    def _emit_state():
        for g in range(N_GROUPS_B):
            st_ref[g * GROUP_W:(g + 1) * GROUP_W, :] = ht_s[g].T


def _ssd_prompt(xbc, dtr, z, cw, cb, dtb, alog, dsk, nw):
    b, l, _ = xbc.shape
    r = SSD_ROWS
    assert l % r == 0
    tile = lambda n: pl.BlockSpec((None, r, n), lambda b_, c_: (b_, c_, 0))
    full = lambda shape: pl.BlockSpec(shape, lambda b_, c_: (0,) * len(shape))
    return pl.pallas_call(
        _ssd_prompt_kernel,
        grid=(b, l // r),
        in_specs=[
            tile(CONV_DIM),
            pl.BlockSpec((None, 8, CONV_DIM), lambda b_, c_: (b_, jnp.maximum(c_ * (r // 8) - 1, 0), 0)),
            tile(LANES), tile(D_INNER),
            full((CONV_W, CONV_DIM)), full((1, CONV_DIM)), full((1, LANES)), full((1, LANES)),
            full((1, LANES)), full((N_GROUPS_B, 1, GROUP_W)),
        ],
        out_specs=(tile(D_INNER), pl.BlockSpec((None, D_INNER, D_STATE), lambda b_, c_: (b_, 0, 0))),
        out_shape=(jax.ShapeDtypeStruct((b, l, D_INNER), BF16),
                   jax.ShapeDtypeStruct((b, D_INNER, D_STATE), F32)),
        scratch_shapes=[pltpu.VMEM((8 + r, CONV_DIM), F32),
                        pltpu.VMEM((N_GROUPS_B, D_STATE, GROUP_W), F32)] + _SSD_SCRATCH,
        compiler_params=_cparams(("parallel", "arbitrary")),
        name="ssd_prompt",
    )(xbc, xbc, dtr, z, cw, cb, dtb, alog, dsk, nw)


def _ssd_sample_kernel(xp_ref, dtr_ref, z_ref, cw_ref, cb_ref, dtb_ref, alog_ref, dsk_ref, nw_ref,
                       h0_ref, o_ref, st_ref, *scr, seq_len):
    g = pl.program_id(1)
    r = SSD_ROWS
    n_seq = r // seq_len

    @pl.when(g == 0)
    def _prepare():
        xp_rows = lambda j: xp_ref[:, pl.ds(j, seq_len), :].reshape(r, CONV_DIM)
        _ssd_prepare(xp_rows, dtr_ref[...], cw_ref, cb_ref, dtb_ref, alog_ref, dsk_ref, seq_len,
                     *scr[:8])

    states = [h0_ref[j].reshape(GROUP_W, D_STATE).T for j in range(n_seq)]
    out, new = _ssd_group(g, seq_len, z_ref[...], nw_ref[...], states, *scr)
    o_ref[...] = out.astype(o_ref.dtype)
    for j in range(n_seq):
        st_ref[j] = new[j].T.reshape(HEADS_PER_GROUP, HEADDIM_B, D_STATE)


def _ssd_sample(layer, xp, dtr, z, state_ssm, cw, cb, dtb, alog, dsk, nw):
    s, win, _ = xp.shape
    seq_len = win - (CONV_W - 1)
    r = SSD_ROWS
    n_seq = r // seq_len
    assert seq_len == 8 and s % n_seq == 0
    full = lambda shape: pl.BlockSpec(shape, lambda t_, g_: (0,) * len(shape))
    kern = functools.partial(_ssd_sample_kernel, seq_len=seq_len)
    return pl.pallas_call(
        kern,
        grid=(s // n_seq, N_GROUPS_B),
        in_specs=[
            pl.BlockSpec((n_seq, win, CONV_DIM), lambda t_, g_: (t_, 0, 0)),
            pl.BlockSpec((r, LANES), lambda t_, g_: (t_, 0)),
            pl.BlockSpec((r, GROUP_W), lambda t_, g_: (t_, g_)),
            full((CONV_W, CONV_DIM)), full((1, CONV_DIM)), full((1, LANES)), full((1, LANES)),
            full((1, LANES)),
            pl.BlockSpec((None, 1, GROUP_W), lambda t_, g_: (g_, 0, 0)),
            pl.BlockSpec((None, n_seq, HEADS_PER_GROUP, HEADDIM_B, D_STATE),
                         lambda t_, g_: (layer, t_, g_, 0, 0)),
        ],
        out_specs=(pl.BlockSpec((r, GROUP_W), lambda t_, g_: (t_, g_)),
                   pl.BlockSpec((n_seq, HEADS_PER_GROUP, HEADDIM_B, D_STATE),
                                lambda t_, g_: (t_, g_, 0, 0))),
        out_shape=(jax.ShapeDtypeStruct((s * seq_len, D_INNER), BF16),
                   jax.ShapeDtypeStruct((s, N_HEADS_B, HEADDIM_B, D_STATE), F32)),
        scratch_shapes=_SSD_SCRATCH,
        compiler_params=_cparams(("parallel", "arbitrary")),
        name="ssd_sample",
    )(xp, dtr, z, cw, cb, dtb, alog, dsk, nw, state_ssm)


def _post_kernel(h_ref, a_ref, s_ref, gate_ref, wa_ref, wb_ref, wo_ref, n1_ref, n2_ref, wg_ref,
                 wu_ref, wd_ref, n3_ref, o_ref):
    ya = _dot(a_ref[...], wa_ref[...])
    yb = _dot(s_ref[...], wb_ref[...])
    merged = (jax.nn.sigmoid(gate_ref[:, 0:D_MODEL]) * ya
              + jax.nn.sigmoid(gate_ref[:, D_MODEL:2 * D_MODEL]) * yb)
    mix = _dot(merged.astype(BF16), wo_ref[...])
    h1 = h_ref[...] + _rms(mix, n1_ref[...])
    u = _rms(h1, n2_ref[...]).astype(BF16)
    ff = None
    step = 1024
    for c in range(0, D_FF, step):
        w = min(step, D_FF - c)
        hid = _silu(_dot(u, wg_ref[:, c:c + w])) * _dot(u, wu_ref[:, c:c + w])
        part = _dot(hid.astype(BF16), wd_ref[c:c + w, :])
        ff = part if ff is None else ff + part
    o_ref[...] = h1 + _rms(ff, n3_ref[...])


def _post(h, attn_o, ssm_o, gates, wa, wb, wo, n1, n2, wg, wu, wd, n3):
    t = h.shape[0]
    tm = min(256, t)
    assert t % tm == 0
    row = lambda n: pl.BlockSpec((tm, n), lambda i: (i, 0))
    return pl.pallas_call(
        _post_kernel,
        grid=(t // tm,),
        in_specs=[row(D_MODEL), row(ATTN_W), row(D_INNER), row(2 * D_MODEL),
                  _resident(wa.shape), _resident(wb.shape), _resident(wo.shape),
                  _resident((1, D_MODEL)), _resident((1, D_MODEL)),
                  _resident(wg.shape), _resident(wu.shape), _resident(wd.shape),
                  _resident((1, D_MODEL))],
        out_specs=row(D_MODEL),
        out_shape=jax.ShapeDtypeStruct((t, D_MODEL), F32),
        compiler_params=_cparams(("parallel",)),
        name="post",
    )(h, attn_o, ssm_o, gates, wa, wb, wo, n1, n2, wg, wu, wd, n3)


def _pad_lanes(x):
    return jnp.pad(x.reshape(1, -1), ((0, 0), (0, LANES - x.shape[-1])))


def kernel(x_prompt, x_sample, cache_k, cache_v, state_ssm, state_conv, page_table, rel_bias,
           norm_mix_pre, w_in, lambda_q1, lambda_k1, lambda_q2, lambda_k2, attn_subln, conv_w,
           conv_b, dt_bias, a_log, d_skip, ssm_norm, w_br_a, w_br_b, w_out, norm_mix_post,
           norm_ffn_pre, w_ffn_gate, w_ffn_up, w_ffn_down, norm_ffn_post):
    bsz, seq, _ = x_prompt.shape
    n_smp, n_new, _ = x_sample.shape
    depth = w_in.shape[0]
    hp = x_prompt.reshape(bsz * seq, D_MODEL)
    hs = x_sample.reshape(n_smp * n_new, D_MODEL)
    row = lambda x: x.reshape(1, -1)
    outs = [[] for _ in range(8)]
    for l in range(depth):
        lam_init = 0.8 - 0.6 * math.exp(-0.3 * l)
        wl = w_in[l]
        w_proj = jnp.concatenate([
            wl[:, 0:ATTN_W] * (ATTN_SCALE * LOG2E),
            wl[:, ATTN_W:8192], wl[:, 8224:10272], wl[:, 8192:8224],
            jnp.zeros((D_MODEL, LANES - N_HEADS_B), F32)], axis=1).astype(BF16)
        lams = (row(lambda_q1[l]), row(lambda_k1[l]), row(lambda_q2[l]), row(lambda_k2[l]))
        ssd_w = (conv_w[l], row(conv_b[l]), _pad_lanes(dt_bias[l]), _pad_lanes(a_log[l]),
                 _pad_lanes(d_skip[l]), ssm_norm[l].reshape(N_GROUPS_B, 1, GROUP_W))
        post_w = (w_br_a[l].astype(BF16), w_br_b[l].astype(BF16), w_out[l].astype(BF16),
                  row(norm_mix_post[l]), row(norm_ffn_pre[l]), w_ffn_gate[l].astype(BF16),
                  w_ffn_up[l].astype(BF16), w_ffn_down[l].astype(BF16), row(norm_ffn_post[l]))
        subln = row(attn_subln[l])

        q, k, v, kb, vb, z, xbc, gates, dtr = _proj(hp, row(norm_mix_pre[l]), w_proj)
        b3 = lambda x: x.reshape(bsz, seq, x.shape[-1])
        attn_o = _attn_prompt(b3(q), b3(kb), b3(vb), rel_bias, lams, subln, lam_init)
        ssm_o, st = _ssd_prompt(b3(xbc), b3(dtr), b3(z), *ssd_w)
        hp = _post(hp, attn_o.reshape(bsz * seq, ATTN_W), ssm_o.reshape(bsz * seq, D_INNER), gates,
                   *post_w)
        outs[0].append(k.reshape(bsz, seq, N_HEADS_A, DV_A))
        outs[1].append(v.reshape(bsz, seq, N_HEADS_A, DV_A))
        outs[2].append(st.reshape(bsz, N_HEADS_B, HEADDIM_B, D_STATE))
        outs[3].append(b3(xbc)[:, seq - (CONV_W - 1):, :])

        q, k, v, kb, vb, z, xbc, gates, dtr = _proj(hs, row(norm_mix_pre[l]), w_proj)
        k4 = k.reshape(n_smp, n_new, N_HEADS_A, DV_A)
        v4 = v.reshape(n_smp, n_new, N_HEADS_A, DV_A)
        attn_o = _attn_sample(l, q.reshape(n_smp, n_new, ATTN_W), k4, v4, cache_k, cache_v,
                              page_table, rel_bias, lams, subln, lam_init)
        xp = jnp.concatenate([state_conv[l], xbc.reshape(n_smp, n_new, CONV_DIM)], axis=1)
        ssm_o, st = _ssd_sample(l, xp, dtr, z, state_ssm, *ssd_w)
        hs = _post(hs, attn_o, ssm_o, gates, *post_w)
        outs[4].append(k4)
        outs[5].append(v4)
        outs[6].append(st)
        outs[7].append(xp[:, n_new:, :])

    stk = [jnp.stack(o) for o in outs]
    return (hp.reshape(bsz, seq, D_MODEL), hs.reshape(n_smp, n_new, D_MODEL),
            stk[0], stk[1], stk[2], stk[3], stk[4], stk[5], stk[6], stk[7])
```

```python
import functools
import math

import jax
import jax.numpy as jnp
from jax import lax
from jax.experimental import pallas as pl
from jax.experimental.pallas import tpu as pltpu

F32 = jnp.float32
BF16 = jnp.bfloat16

D_MODEL = 1024
N_HEADS_A = 8
HD_A = 64
DV_A = 128
ATTN_W = N_HEADS_A * DV_A
ATTN_SCALE = HD_A ** -0.5
LOG2E = math.log2(math.e)
N_BUCKETS = 32
MAX_EXACT = 16
MAX_DISTANCE = 128
D_INNER = 2048
HEADDIM_B = 64
N_HEADS_B = 32
N_GROUPS_B = 4
HEADS_PER_GROUP = N_HEADS_B // N_GROUPS_B
GROUP_W = D_INNER // N_GROUPS_B
D_STATE = 128
CONV_W = 4
CONV_DIM = D_INNER + 2 * N_GROUPS_B * D_STATE
SSD_ROWS = 128
D_FF = 2816
EPS = 1e-6
LANES = 128
NEG_BIG = -1e30
BIAS_BLOCK = 128
VMEM_LIMIT = 56 * 1024 * 1024

C_Q, C_K, C_V = 0, 1024, 2048
C_Z = 3072
C_XBC = C_Z + D_INNER
C_GATE = C_XBC + CONV_DIM
C_DT = C_GATE + 2 * D_MODEL
IN_COLS_PAD = C_DT + LANES


def _cparams(sem):
    return pltpu.CompilerParams(dimension_semantics=sem, vmem_limit_bytes=VMEM_LIMIT)


def _resident(shape):
    nd = len(shape)
    return pl.BlockSpec(shape, lambda *_: (0,) * nd, pipeline_mode=pl.Buffered(1))


def _rms(x, g):
    ms = jnp.mean(x * x, axis=-1, keepdims=True)
    return x * lax.rsqrt(ms + EPS) * g


def _silu(x):
    return x * jax.nn.sigmoid(x)


def _dot(a, b):
    return jnp.dot(a, b, preferred_element_type=F32)


def _dot_nt(a, b):
    return lax.dot_general(a, b, (((1,), (1,)), ((), ())), preferred_element_type=F32)


def _dot_tn(a, b):
    return lax.dot_general(a, b, (((0,), (0,)), ((), ())), preferred_element_type=F32)


def _t5_bias(dist, rb_of_bucket):
    dist = jnp.maximum(dist, 0)
    d = jnp.maximum(dist, MAX_EXACT).astype(F32)
    large = MAX_EXACT + (jnp.log(d / MAX_EXACT) / math.log(MAX_DISTANCE / MAX_EXACT)
                         * (N_BUCKETS - MAX_EXACT)).astype(jnp.int32)
    large = jnp.minimum(large, N_BUCKETS - 1)
    bucket = jnp.where(dist < MAX_EXACT, dist, large)
    far = rb_of_bucket(N_BUCKETS - 1)
    out = jnp.zeros(dist.shape, F32)
    for j in range(N_BUCKETS - 1):
        out = jnp.where(bucket == j, rb_of_bucket(j) - far, out)
    return out * LOG2E


def _diff_lambda(lq1, lk1, lq2, lk2, lam_init):
    s1 = jnp.sum(lq1[...] * lk1[...], axis=-1, keepdims=True)
    s2 = jnp.sum(lq2[...] * lk2[...], axis=-1, keepdims=True)
    return jnp.exp(s1) - jnp.exp(s2) + lam_init


def _proj_kernel(h_ref, g_ref, w_ref, q_ref, k_ref, v_ref, kb_ref, vb_ref, z_ref, xbc_ref,
                 gate_ref, dt_ref):
    u = _rms(h_ref[...], g_ref[...]).astype(BF16)
    step = 512

    def cols(c0, n):
        for c in range(0, n, step):
            w = min(step, n - c)
            yield c, w, _dot(u, w_ref[:, c0 + c:c0 + c + w])

    for c, w, r in cols(C_Q, ATTN_W):
        q_ref[:, c:c + w] = r.astype(BF16)
    for c, w, r in cols(C_K, ATTN_W):
        k_ref[:, c:c + w] = r
        kb_ref[:, c:c + w] = r.astype(BF16)
    for c, w, r in cols(C_V, ATTN_W):
        v_ref[:, c:c + w] = r
        vb_ref[:, c:c + w] = r.astype(BF16)
    for c, w, r in cols(C_Z, D_INNER):
        z_ref[:, c:c + w] = r
    for c, w, r in cols(C_XBC, CONV_DIM):
        xbc_ref[:, c:c + w] = r
    for c, w, r in cols(C_GATE, 2 * D_MODEL):
        gate_ref[:, c:c + w] = r
    for c, w, r in cols(C_DT, LANES):
        dt_ref[:, c:c + w] = r


def _proj(h, gain, w):
    t = h.shape[0]
    tm = min(256, t)
    assert t % tm == 0
    row = lambda n: pl.BlockSpec((tm, n), lambda i: (i, 0))
    out_shape = (
        jax.ShapeDtypeStruct((t, ATTN_W), BF16),
        jax.ShapeDtypeStruct((t, ATTN_W), F32),
        jax.ShapeDtypeStruct((t, ATTN_W), F32),
        jax.ShapeDtypeStruct((t, ATTN_W), BF16),
        jax.ShapeDtypeStruct((t, ATTN_W), BF16),
        jax.ShapeDtypeStruct((t, D_INNER), F32),
        jax.ShapeDtypeStruct((t, CONV_DIM), F32),
        jax.ShapeDtypeStruct((t, 2 * D_MODEL), F32),
        jax.ShapeDtypeStruct((t, LANES), F32),
    )
    return pl.pallas_call(
        _proj_kernel,
        grid=(t // tm,),
        in_specs=[row(D_MODEL), _resident((1, D_MODEL)), _resident((D_MODEL, IN_COLS_PAD))],
        out_specs=tuple(row(s.shape[1]) for s in out_shape),
        out_shape=out_shape,
        compiler_params=_cparams(("parallel",)),
        name="proj",
    )(h, gain, w)


def _attn_prompt_kernel(rb_ref, lq1, lk1, lq2, lk2, subln_ref, q_ref, k_ref, v_ref, o_ref,
                        qst_ref, vt_ref, m_ref, l_ref, acc_ref, bias_ref, *, tq, tk, lam_init,
                        far_unroll):
    h = pl.program_id(1)
    i = pl.program_id(2)
    n_kt = k_ref.shape[0] // tk
    ratio = tq // tk

    @pl.when(i == 0)
    def _per_head_setup():
        blk = BIAS_BLOCK
        nbq, nbk = tq // blk, tk // blk
        r = lax.broadcasted_iota(jnp.int32, (blk, blk), 0)
        c = lax.broadcasted_iota(jnp.int32, (blk, blk), 1)
        rb = lambda j: rb_ref[j, h]
        on_diag = jnp.where(c - r >= 0, _t5_bias(c - r, rb), NEG_BIG)
        after_diag = _t5_bias(c - r + blk, rb)
        masked = jnp.full((blk, blk), NEG_BIG, F32)
        zero = jnp.zeros((blk, blk), F32)

        def block(delta):
            return masked if delta < 0 else on_diag if delta == 0 else after_diag if delta == 1 else zero

        for w in range(ratio + 1):
            for a in range(nbk):
                row = [block(b - a + (1 - w) * nbk) for b in range(nbq)]
                bias_ref[w, a * blk:(a + 1) * blk, :] = jnp.concatenate(row + row, axis=1)

        def transpose_v(j, carry):
            st = pl.multiple_of(j * tk, tk)
            vt_ref[j] = v_ref[pl.ds(st, tk), :].astype(F32).T.astype(BF16)
            return carry

        lax.fori_loop(0, n_kt, transpose_v, 0)

    qt = q_ref[...].astype(F32).T
    d_idx = lax.broadcasted_iota(jnp.int32, qt.shape, 0)
    qst_ref[...] = jnp.concatenate(
        [jnp.where(d_idx < HD_A, qt, 0.0), jnp.where(d_idx >= HD_A, qt, 0.0)], axis=1).astype(BF16)
    m_ref[...] = jnp.full(m_ref.shape, NEG_BIG, F32)
    l_ref[...] = jnp.zeros(l_ref.shape, F32)
    acc_ref[...] = jnp.zeros(acc_ref.shape, F32)

    def update(tiles):
        qst = qst_ref[...]
        scores = [_dot(k_ref[pl.ds(pl.multiple_of(j * tk, tk), tk), :], qst) for j, _ in tiles]
        for s, (j, bias) in zip(scores, tiles):
            if bias is not None:
                s = s + bias
            m_prev = m_ref[...]
            m_new = jnp.maximum(m_prev, jnp.max(s, axis=0, keepdims=True))
            alpha = jnp.exp2(m_prev - m_new)
            p = jnp.exp2(s - m_new[0:1, :])
            l_ref[...] = alpha * l_ref[...] + jnp.sum(p, axis=0, keepdims=True)
            m_ref[...] = m_new
            acc_ref[...] = alpha[0:1, :] * acc_ref[...] + _dot(vt_ref[j], p.astype(BF16))

    first_near = i * ratio - 1

    @pl.when(i == 0)
    def _diagonal_only():
        update([(first_near + w, bias_ref[w]) for w in range(1, ratio + 1)])

    @pl.when(i >= 1)
    def _diagonal_and_previous():
        update([(first_near + w, bias_ref[w]) for w in range(ratio + 1)])

    n_far = jnp.maximum(first_near, 0)
    n_quads = n_far // far_unroll

    def far_body(g, carry):
        update([(g * far_unroll + u, None) for u in range(far_unroll)])
        return carry

    lax.fori_loop(0, n_quads, far_body, 0)
    base = n_quads * far_unroll
    rem = n_far - base
    width = far_unroll // 2
    while width >= 1:
        take = (rem & width) != 0

        @pl.when(take)
        def _remainder(base=base, width=width):
            update([(base + u, None) for u in range(width)])

        base = base + jnp.where(take, width, 0)
        width //= 2

    lam = _diff_lambda(lq1, lk1, lq2, lk2, lam_init)
    o = acc_ref[...] / l_ref[0:1, :]
    o = (o[:, 0:tq] - lam * o[:, tq:2 * tq]).T
    o = _rms(o, subln_ref[...]) * (1.0 - lam_init)
    o_ref[...] = o.astype(o_ref.dtype)


def _attn_prompt(q, k, v, rel_bias, lams, subln, lam_init):
    b, l, _ = q.shape
    tq, tk = min(1024, l), min(512, l)
    assert l % tq == 0 and tq % tk == 0 and tk % BIAS_BLOCK == 0 and BIAS_BLOCK >= MAX_DISTANCE
    vec = lambda n: pl.BlockSpec((1, n), lambda b_, h_, i_: (0, 0))
    kern = functools.partial(_attn_prompt_kernel, tq=tq, tk=tk, lam_init=lam_init, far_unroll=4)
    return pl.pallas_call(
        kern,
        grid=(b, N_HEADS_A, l // tq),
        in_specs=[
            pl.BlockSpec(memory_space=pltpu.SMEM),
            vec(HD_A), vec(HD_A), vec(HD_A), vec(HD_A), vec(DV_A),
            pl.BlockSpec((None, tq, DV_A), lambda b_, h_, i_: (b_, i_, h_)),
            pl.BlockSpec((None, l, DV_A), lambda b_, h_, i_: (b_, 0, h_)),
            pl.BlockSpec((None, l, DV_A), lambda b_, h_, i_: (b_, 0, h_)),
        ],
        out_specs=pl.BlockSpec((None, tq, DV_A), lambda b_, h_, i_: (b_, i_, h_)),
        out_shape=jax.ShapeDtypeStruct((b, l, ATTN_W), BF16),
        scratch_shapes=[
            pltpu.VMEM((DV_A, 2 * tq), BF16),
            pltpu.VMEM((l // tk, DV_A, tk), BF16),
            pltpu.VMEM((8, 2 * tq), F32),
            pltpu.VMEM((8, 2 * tq), F32),
            pltpu.VMEM((DV_A, 2 * tq), F32),
            pltpu.VMEM((tq // tk + 1, tk, 2 * tq), F32),
        ],
        compiler_params=_cparams(("parallel", "parallel", "arbitrary")),
        name="attn_prompt",
    )(rel_bias, *lams, subln, q, k, v)


PAGES_PER_STEP = 8
PAGES_PER_TILE = 2


def _attn_sample_kernel(pt_ref, rbx_ref, lq1, lk1, lq2, lk2, subln_ref, qt_ref, *rest,
                        page, n_new, lam_init):
    del pt_ref
    n_pp = PAGES_PER_STEP
    kc_refs, vc_refs = rest[:n_pp], rest[n_pp:2 * n_pp]
    kn_ref, vn_ref, o_ref, m_ref, l_ref, acc_ref, bias_ref, biasn_ref = rest[2 * n_pp:]
    b = pl.program_id(0)
    p = pl.program_id(1)
    last = p == pl.num_programs(1) - 1
    nh = N_HEADS_A
    rows = 2 * nh * n_new
    tile_tok = PAGES_PER_TILE * page

    @pl.when((b == 0) & (p == 0))
    def _build_bias():
        rb = lambda j: rbx_ref[:, j:j + 1]
        r = lax.broadcasted_iota(jnp.int32, (rows, tile_tok), 0)
        tok = lax.broadcasted_iota(jnp.int32, (rows, tile_tok), 1)
        bias_ref[...] = _t5_bias(tile_tok + r % n_new - tok, rb)
        r = lax.broadcasted_iota(jnp.int32, (rows, page), 0)
        tok = lax.broadcasted_iota(jnp.int32, (rows, page), 1)
        dist = r % n_new - tok
        biasn_ref[...] = jnp.where(dist >= 0, _t5_bias(dist, rb), NEG_BIG)

    @pl.when(p == 0)
    def _init():
        m_ref[...] = jnp.full(m_ref.shape, NEG_BIG, F32)
        l_ref[...] = jnp.zeros(l_ref.shape, F32)
        acc_ref[...] = jnp.zeros(acc_ref.shape, F32)

    qt = qt_ref[...]

    def flat(refs):
        parts = []
        for ref in refs:
            parts.append(jnp.concatenate(
                [ref[pl.ds(h, page, stride=nh), :].astype(BF16) for h in range(nh)], axis=1))
        return parts[0] if len(parts) == 1 else jnp.concatenate(parts, axis=0)

    def update(tiles):
        scores = [_dot_nt(qt, kf) for kf, _, _ in tiles]
        for st, (_, vf, extra) in zip(scores, tiles):
            n_tok = st.shape[1]
            if extra is not None:
                st = st + extra
            m_prev = m_ref[...]
            m_new = jnp.maximum(m_prev, jnp.max(st, axis=1, keepdims=True))
            alpha = jnp.exp2(m_prev - m_new)
            pr = jnp.exp2(st - jnp.tile(m_new, (1, n_tok // LANES)))
            l_ref[...] = alpha * l_ref[...] + jnp.sum(pr, axis=1, keepdims=True)
            m_ref[...] = m_new
            o_all = _dot(pr.astype(BF16), vf)
            per_head = rows // nh
            pv = jnp.concatenate(
                [o_all[h * per_head:(h + 1) * per_head, h * DV_A:(h + 1) * DV_A] for h in range(nh)],
                axis=0)
            acc_ref[...] = alpha * acc_ref[...] + pv

    n_tiles = PAGES_PER_STEP // PAGES_PER_TILE
    tiles = []
    for t in range(n_tiles):
        sl = slice(t * PAGES_PER_TILE, (t + 1) * PAGES_PER_TILE)
        extra = None
        if t == n_tiles - 1:
            extra = jnp.where(last, bias_ref[...], 0.0)
        tiles.append((flat(kc_refs[sl]), flat(vc_refs[sl]), extra))
    update(tiles)

    @pl.when(last)
    def _new_tokens_and_finish():
        update([(flat([kn_ref]), flat([vn_ref]), biasn_ref[...])])
        lam = _diff_lambda(lq1, lk1, lq2, lk2, lam_init)
        on = (acc_ref[...] / l_ref[...]).reshape(nh, 2, n_new, DV_A)
        o = on[:, 0] - lam * on[:, 1]
        o_ref[...] = _rms(o, subln_ref[...]) * (1.0 - lam_init)


def _attn_sample(layer, q, k_new, v_new, cache_k, cache_v, page_table, rel_bias, lams, subln,
                 lam_init):
    s, n_new, _ = q.shape
    n_pages = page_table.shape[1]
    page = cache_k.shape[2]
    nh = N_HEADS_A
    rows = 2 * nh * n_new
    assert rows == LANES and page == LANES and n_pages % PAGES_PER_STEP == 0
    assert PAGES_PER_TILE * page - page + 1 >= MAX_DISTANCE
    qh = jnp.transpose(q.reshape(s, n_new, nh, DV_A), (0, 2, 1, 3))
    half = jnp.arange(2)[:, None, None] == (jnp.arange(DV_A) // HD_A)[None, None, :]
    zero = jnp.zeros((), q.dtype)
    qm = jnp.where(half[None, None], qh[:, :, None], zero)
    eye = jnp.eye(nh, dtype=bool)
    qt = jnp.where(eye[None, :, None, None, :, None], qm[:, :, :, :, None, :], zero)
    qt = qt.reshape(s, rows, nh * DV_A)
    rbx = jnp.repeat(rel_bias.T, 2 * n_new, axis=0)
    pad = ((0, 0), (0, page - n_new), (0, 0), (0, 0))
    k_pad = jnp.pad(k_new, pad).reshape(s, page * nh, DV_A)
    v_pad = jnp.pad(v_new, pad).reshape(s, page * nh, DV_A)
    cache_k = cache_k.reshape(cache_k.shape[:2] + (page * nh, DV_A))
    cache_v = cache_v.reshape(cache_v.shape[:2] + (page * nh, DV_A))
    vec = lambda n: pl.BlockSpec((1, n), lambda b_, p_, pt: (0, 0))

    def page_spec(u):
        return pl.BlockSpec((None, None, page * nh, DV_A),
                            lambda b_, p_, pt: (layer, pt[b_, p_ * PAGES_PER_STEP + u], 0, 0))

    new_spec = pl.BlockSpec((None, page * nh, DV_A), lambda b_, p_, pt: (b_, 0, 0))
    kern = functools.partial(_attn_sample_kernel, page=page, n_new=n_new, lam_init=lam_init)
    grid_spec = pltpu.PrefetchScalarGridSpec(
        num_scalar_prefetch=1,
        grid=(s, n_pages // PAGES_PER_STEP),
        in_specs=[
            pl.BlockSpec((rows, N_BUCKETS), lambda b_, p_, pt: (0, 0)),
            vec(HD_A), vec(HD_A), vec(HD_A), vec(HD_A), vec(DV_A),
            pl.BlockSpec((None, rows, nh * DV_A), lambda b_, p_, pt: (b_, 0, 0)),
            *[page_spec(u) for u in range(PAGES_PER_STEP)],
            *[page_spec(u) for u in range(PAGES_PER_STEP)],
            new_spec, new_spec,
        ],
        out_specs=pl.BlockSpec((None, nh, n_new, DV_A), lambda b_, p_, pt: (b_, 0, 0, 0)),
        scratch_shapes=[
            pltpu.VMEM((rows, LANES), F32),
            pltpu.VMEM((rows, LANES), F32),
            pltpu.VMEM((rows, DV_A), F32),
            pltpu.VMEM((rows, PAGES_PER_TILE * page), F32),
            pltpu.VMEM((rows, page), F32),
        ],
    )
    o = pl.pallas_call(
        kern,
        grid_spec=grid_spec,
        out_shape=jax.ShapeDtypeStruct((s, nh, n_new, DV_A), F32),
        compiler_params=_cparams(("arbitrary", "arbitrary")),
        name="attn_sample",
    )(page_table, rbx, *lams, subln, qt, *([cache_k] * PAGES_PER_STEP),
      *([cache_v] * PAGES_PER_STEP), k_pad, v_pad)
    return jnp.transpose(o, (0, 2, 1, 3)).reshape(s * n_new, ATTN_W).astype(BF16)


def _ssd_prepare(xp_rows, dtr, cw_ref, cb_ref, dtb_ref, alog_ref, dsk_ref, seq_len,
                 xs_s, bm_s, cm_s, dt_s, acs_s, aend_s, acst_s, dsk_s):
    r = SSD_ROWS
    y = cb_ref[...] + xp_rows(0) * cw_ref[0:1, :]
    for j in range(1, CONV_W):
        y = y + xp_rows(j) * cw_ref[j:j + 1, :]
    xc = _silu(y)
    dt = jax.nn.softplus(dtr + dtb_ref[...])
    a = -jnp.exp(alog_ref[...])
    dta = dt * a
    row = lax.broadcasted_iota(jnp.int32, (r, r), 0)
    col = lax.broadcasted_iota(jnp.int32, (r, r), 1)
    same = (row // seq_len) == (col // seq_len)
    tri = (same & (col <= row)).astype(F32)
    a_cs = jnp.dot(tri, dta, precision=lax.Precision.HIGHEST, preferred_element_type=F32)
    a_end = jnp.dot(same.astype(F32), dta, precision=lax.Precision.HIGHEST,
                    preferred_element_type=F32)
    a_cs_t = a_cs.T
    dsk = jnp.broadcast_to(dsk_ref[...], (8, LANES))
    for g in range(N_GROUPS_B):
        xs_s[g] = xc[:, g * GROUP_W:(g + 1) * GROUP_W]
        bm_s[g] = xc[:, D_INNER + g * D_STATE:D_INNER + (g + 1) * D_STATE].astype(BF16)
        c0 = D_INNER + N_GROUPS_B * D_STATE + g * D_STATE
        cm_s[g] = xc[:, c0:c0 + D_STATE].astype(BF16)
        shift = (LANES - HEADS_PER_GROUP * g) % LANES
        roll = (lambda x: x) if shift == 0 else (lambda x: pltpu.roll(x, shift, 1))
        dt_s[g] = roll(dt)
        acs_s[g] = roll(a_cs)
        aend_s[g] = roll(a_end)
        dsk_s[g] = roll(dsk)
        acst_s[g] = a_cs_t[g * HEADS_PER_GROUP:(g + 1) * HEADS_PER_GROUP, :]


def _ssd_group(g, seq_len, z_g, nw_g, state_t, xs_s, bm_s, cm_s, dt_s, acs_s, aend_s, acst_s,
               dsk_s):
    r = SSD_ROWS
    n_seq = r // seq_len
    xs_g = xs_s[g]
    bmb = bm_s[g]
    cmb = cm_s[g]
    dtg = dt_s[g]
    acs = acs_s[g]
    aend = aend_s[g]
    acst = acst_s[g]
    dsk = dsk_s[g]
    aend = aend[0:1, :] if n_seq == 1 else aend

    row = lax.broadcasted_iota(jnp.int32, (r, r), 0)
    col = lax.broadcasted_iota(jnp.int32, (r, r), 1)
    tri = ((row // seq_len) == (col // seq_len)) & (col <= row)
    row_seq = lax.broadcasted_iota(jnp.int32, (r, 1), 0) // seq_len

    cb = _dot_nt(cmb, bmb)

    if n_seq == 1:
        y_off = _dot(cmb, state_t[0].astype(BF16))
    else:
        y_off = jnp.zeros((r, GROUP_W), F32)
        for j in range(n_seq):
            y_off = y_off + jnp.where(row_seq == j, _dot(cmb, state_t[j].astype(BF16)), 0.0)

    left = lax.broadcasted_iota(jnp.int32, (1, LANES), 1) < HEADDIM_B
    gz_parts, xw_parts, dec_parts = [], [], []

    for pp in range(HEADS_PER_GROUP // 2):
        h0, h1 = 2 * pp, 2 * pp + 1
        lo, hi = pp * LANES, (pp + 1) * LANES

        def expand2(arr):
            return jnp.where(left, arr[:, h0:h0 + 1], arr[:, h1:h1 + 1])

        cols = [jnp.broadcast_to(acs[:, hh:hh + 1], (r, LANES)) for hh in (h0, h1)]
        acs2 = jnp.where(left, cols[0], cols[1])
        aend2 = expand2(aend)
        xs_p = xs_g[:, lo:hi]
        xdt = xs_p * expand2(dtg)
        xdt_b = xdt.astype(BF16)
        zero_b = jnp.zeros_like(xdt_b)
        y = y_off[:, lo:hi] * jnp.exp(acs2) + expand2(dsk[0:1, :]) * xs_p
        for hh, keep, colb in ((h0, left, cols[0]), (h1, jnp.logical_not(left), cols[1])):
            seg = colb - acst[hh:hh + 1, :]
            decay = jnp.exp(jnp.where(tri, seg, -jnp.inf))
            gm = (cb * decay).astype(BF16)
            y = y + _dot(gm, jnp.where(keep, xdt_b, zero_b))
        gz_parts.append(y * _silu(z_g[:, lo:hi]))
        xw_parts.append((xdt * jnp.exp(aend2 - acs2)).astype(BF16))
        dec_parts.append(jnp.exp(aend2))

    gz = jnp.concatenate(gz_parts, axis=1)
    ms = jnp.mean(gz * gz, axis=-1, keepdims=True)
    out = gz * lax.rsqrt(ms + EPS) * nw_g

    xw = jnp.concatenate(xw_parts, axis=1)
    dec = jnp.concatenate(dec_parts, axis=1)
    new_states = []
    for j in range(n_seq):
        if n_seq == 1:
            bsel = bmb
        else:
            bsel = jnp.where(row_seq == j, bmb, jnp.zeros_like(bmb))
        new_states.append(state_t[j] * dec[j * seq_len:j * seq_len + 1, :] + _dot_tn(bsel, xw))
    return out, new_states


_SSD_SCRATCH = [
    pltpu.VMEM((N_GROUPS_B, SSD_ROWS, GROUP_W), F32),
    pltpu.VMEM((N_GROUPS_B, SSD_ROWS, D_STATE), BF16),
    pltpu.VMEM((N_GROUPS_B, SSD_ROWS, D_STATE), BF16),
    pltpu.VMEM((N_GROUPS_B, SSD_ROWS, LANES), F32),
    pltpu.VMEM((N_GROUPS_B, SSD_ROWS, LANES), F32),
    pltpu.VMEM((N_GROUPS_B, SSD_ROWS, LANES), F32),
    pltpu.VMEM((N_GROUPS_B, HEADS_PER_GROUP, SSD_ROWS), F32),
    pltpu.VMEM((N_GROUPS_B, 8, LANES), F32),
]


def _ssd_prompt_kernel(cur_ref, hist_ref, dtr_ref, z_ref, cw_ref, cb_ref, dtb_ref, alog_ref,
                       dsk_ref, nw_ref, o_ref, st_ref, xp_s, ht_s, *scr):
    c = pl.program_id(1)
    n_c = pl.num_programs(1)
    r = SSD_ROWS

    @pl.when(c == 0)
    def _start():
        ht_s[...] = jnp.zeros(ht_s.shape, F32)
        xp_s[0:8, :] = jnp.zeros((8, CONV_DIM), F32)

    @pl.when(c > 0)
    def _history():
        xp_s[0:8, :] = hist_ref[...]

    xp_s[8:8 + r, :] = cur_ref[...]
    xp_rows = lambda j: xp_s[pl.ds(8 - (CONV_W - 1) + j, r), :]
    _ssd_prepare(xp_rows, dtr_ref[...], cw_ref, cb_ref, dtb_ref, alog_ref, dsk_ref, r, *scr[:8])
    for g in range(N_GROUPS_B):
        out, new = _ssd_group(g, r, z_ref[:, g * GROUP_W:(g + 1) * GROUP_W], nw_ref[g],
                              [ht_s[g]], *scr)
        o_ref[:, g * GROUP_W:(g + 1) * GROUP_W] = out.astype(o_ref.dtype)
        ht_s[g] = new[0]

    @pl.when(c == n_c - 1)
    def _emit_state():
        for g in range(N_GROUPS_B):
            st_ref[g * GROUP_W:(g + 1) * GROUP_W, :] = ht_s[g].T


def _ssd_prompt(xbc, dtr, z, cw, cb, dtb, alog, dsk, nw):
    b, l, _ = xbc.shape
    r = SSD_ROWS
    assert l % r == 0
    tile = lambda n: pl.BlockSpec((None, r, n), lambda b_, c_: (b_, c_, 0))
    full = lambda shape: pl.BlockSpec(shape, lambda b_, c_: (0,) * len(shape))
    return pl.pallas_call(
        _ssd_prompt_kernel,
        grid=(b, l // r),
        in_specs=[
            tile(CONV_DIM),
            pl.BlockSpec((None, 8, CONV_DIM), lambda b_, c_: (b_, jnp.maximum(c_ * (r // 8) - 1, 0), 0)),
            tile(LANES), tile(D_INNER),
            full((CONV_W, CONV_DIM)), full((1, CONV_DIM)), full((1, LANES)), full((1, LANES)),
            full((1, LANES)), full((N_GROUPS_B, 1, GROUP_W)),
        ],
        out_specs=(tile(D_INNER), pl.BlockSpec((None, D_INNER, D_STATE), lambda b_, c_: (b_, 0, 0))),
        out_shape=(jax.ShapeDtypeStruct((b, l, D_INNER), BF16),
                   jax.ShapeDtypeStruct((b, D_INNER, D_STATE), F32)),
        scratch_shapes=[pltpu.VMEM((8 + r, CONV_DIM), F32),
                        pltpu.VMEM((N_GROUPS_B, D_STATE, GROUP_W), F32)] + _SSD_SCRATCH,
        compiler_params=_cparams(("parallel", "arbitrary")),
        name="ssd_prompt",
    )(xbc, xbc, dtr, z, cw, cb, dtb, alog, dsk, nw)


def _ssd_sample_kernel(xp_ref, dtr_ref, z_ref, cw_ref, cb_ref, dtb_ref, alog_ref, dsk_ref, nw_ref,
                       h0_ref, o_ref, st_ref, *scr, seq_len):
    g = pl.program_id(1)
    r = SSD_ROWS
    n_seq = r // seq_len

    @pl.when(g == 0)
    def _prepare():
        xp_rows = lambda j: xp_ref[:, pl.ds(j, seq_len), :].reshape(r, CONV_DIM)
        _ssd_prepare(xp_rows, dtr_ref[...], cw_ref, cb_ref, dtb_ref, alog_ref, dsk_ref, seq_len,
                     *scr[:8])

    states = [h0_ref[j].reshape(GROUP_W, D_STATE).T for j in range(n_seq)]
    out, new = _ssd_group(g, seq_len, z_ref[...], nw_ref[...], states, *scr)
    o_ref[...] = out.astype(o_ref.dtype)
    for j in range(n_seq):
        st_ref[j] = new[j].T.reshape(HEADS_PER_GROUP, HEADDIM_B, D_STATE)


def _ssd_sample(layer, xp, dtr, z, state_ssm, cw, cb, dtb, alog, dsk, nw):
    s, win, _ = xp.shape
    seq_len = win - (CONV_W - 1)
    r = SSD_ROWS
    n_seq = r // seq_len
    assert seq_len == 8 and s % n_seq == 0
    full = lambda shape: pl.BlockSpec(shape, lambda t_, g_: (0,) * len(shape))
    kern = functools.partial(_ssd_sample_kernel, seq_len=seq_len)
    return pl.pallas_call(
        kern,
        grid=(s // n_seq, N_GROUPS_B),
        in_specs=[
            pl.BlockSpec((n_seq, win, CONV_DIM), lambda t_, g_: (t_, 0, 0)),
            pl.BlockSpec((r, LANES), lambda t_, g_: (t_, 0)),
            pl.BlockSpec((r, GROUP_W), lambda t_, g_: (t_, g_)),
            full((CONV_W, CONV_DIM)), full((1, CONV_DIM)), full((1, LANES)), full((1, LANES)),
            full((1, LANES)),
            pl.BlockSpec((None, 1, GROUP_W), lambda t_, g_: (g_, 0, 0)),
            pl.BlockSpec((None, n_seq, HEADS_PER_GROUP, HEADDIM_B, D_STATE),
                         lambda t_, g_: (layer, t_, g_, 0, 0)),
        ],
        out_specs=(pl.BlockSpec((r, GROUP_W), lambda t_, g_: (t_, g_)),
                   pl.BlockSpec((n_seq, HEADS_PER_GROUP, HEADDIM_B, D_STATE),
                                lambda t_, g_: (t_, g_, 0, 0))),
        out_shape=(jax.ShapeDtypeStruct((s * seq_len, D_INNER), BF16),
                   jax.ShapeDtypeStruct((s, N_HEADS_B, HEADDIM_B, D_STATE), F32)),
        scratch_shapes=_SSD_SCRATCH,
        compiler_params=_cparams(("parallel", "arbitrary")),
        name="ssd_sample",
    )(xp, dtr, z, cw, cb, dtb, alog, dsk, nw, state_ssm)


def _post_kernel(h_ref, a_ref, s_ref, gate_ref, wa_ref, wb_ref, wo_ref, n1_ref, n2_ref, wg_ref,
                 wu_ref, wd_ref, n3_ref, o_ref):
    ya = _dot(a_ref[...], wa_ref[...])
    yb = _dot(s_ref[...], wb_ref[...])
    merged = (jax.nn.sigmoid(gate_ref[:, 0:D_MODEL]) * ya
              + jax.nn.sigmoid(gate_ref[:, D_MODEL:2 * D_MODEL]) * yb)
    mix = _dot(merged.astype(BF16), wo_ref[...])
    h1 = h_ref[...] + _rms(mix, n1_ref[...])
    u = _rms(h1, n2_ref[...]).astype(BF16)
    ff = None
    step = 1024
    for c in range(0, D_FF, step):
        w = min(step, D_FF - c)
        hid = _silu(_dot(u, wg_ref[:, c:c + w])) * _dot(u, wu_ref[:, c:c + w])
        part = _dot(hid.astype(BF16), wd_ref[c:c + w, :])
        ff = part if ff is None else ff + part
    o_ref[...] = h1 + _rms(ff, n3_ref[...])


def _post(h, attn_o, ssm_o, gates, wa, wb, wo, n1, n2, wg, wu, wd, n3):
    t = h.shape[0]
    tm = min(256, t)
    assert t % tm == 0
    row = lambda n: pl.BlockSpec((tm, n), lambda i: (i, 0))
    return pl.pallas_call(
        _post_kernel,
        grid=(t // tm,),
        in_specs=[row(D_MODEL), row(ATTN_W), row(D_INNER), row(2 * D_MODEL),
                  _resident(wa.shape), _resident(wb.shape), _resident(wo.shape),
                  _resident((1, D_MODEL)), _resident((1, D_MODEL)),
                  _resident(wg.shape), _resident(wu.shape), _resident(wd.shape),
                  _resident((1, D_MODEL))],
        out_specs=row(D_MODEL),
        out_shape=jax.ShapeDtypeStruct((t, D_MODEL), F32),
        compiler_params=_cparams(("parallel",)),
        name="post",
    )(h, attn_o, ssm_o, gates, wa, wb, wo, n1, n2, wg, wu, wd, n3)


def _pad_lanes(x):
    return jnp.pad(x.reshape(1, -1), ((0, 0), (0, LANES - x.shape[-1])))


def kernel(x_prompt, x_sample, cache_k, cache_v, state_ssm, state_conv, page_table, rel_bias,
           norm_mix_pre, w_in, lambda_q1, lambda_k1, lambda_q2, lambda_k2, attn_subln, conv_w,
           conv_b, dt_bias, a_log, d_skip, ssm_norm, w_br_a, w_br_b, w_out, norm_mix_post,
           norm_ffn_pre, w_ffn_gate, w_ffn_up, w_ffn_down, norm_ffn_post):
    bsz, seq, _ = x_prompt.shape
    n_smp, n_new, _ = x_sample.shape
    depth = w_in.shape[0]
    hp = x_prompt.reshape(bsz * seq, D_MODEL)
    hs = x_sample.reshape(n_smp * n_new, D_MODEL)
    row = lambda x: x.reshape(1, -1)
    outs = [[] for _ in range(8)]
    for l in range(depth):
        lam_init = 0.8 - 0.6 * math.exp(-0.3 * l)
        wl = w_in[l]
        w_proj = jnp.concatenate([
            wl[:, 0:ATTN_W] * (ATTN_SCALE * LOG2E),
            wl[:, ATTN_W:8192], wl[:, 8224:10272], wl[:, 8192:8224],
            jnp.zeros((D_MODEL, LANES - N_HEADS_B), F32)], axis=1).astype(BF16)
        lams = (row(lambda_q1[l]), row(lambda_k1[l]), row(lambda_q2[l]), row(lambda_k2[l]))
        ssd_w = (conv_w[l], row(conv_b[l]), _pad_lanes(dt_bias[l]), _pad_lanes(a_log[l]),
                 _pad_lanes(d_skip[l]), ssm_norm[l].reshape(N_GROUPS_B, 1, GROUP_W))
        post_w = (w_br_a[l].astype(BF16), w_br_b[l].astype(BF16), w_out[l].astype(BF16),
                  row(norm_mix_post[l]), row(norm_ffn_pre[l]), w_ffn_gate[l].astype(BF16),
                  w_ffn_up[l].astype(BF16), w_ffn_down[l].astype(BF16), row(norm_ffn_post[l]))
        subln = row(attn_subln[l])

        q, k, v, kb, vb, z, xbc, gates, dtr = _proj(hp, row(norm_mix_pre[l]), w_proj)
        b3 = lambda x: x.reshape(bsz, seq, x.shape[-1])
        attn_o = _attn_prompt(b3(q), b3(kb), b3(vb), rel_bias, lams, subln, lam_init)
        ssm_o, st = _ssd_prompt(b3(xbc), b3(dtr), b3(z), *ssd_w)
        hp = _post(hp, attn_o.reshape(bsz * seq, ATTN_W), ssm_o.reshape(bsz * seq, D_INNER), gates,
                   *post_w)
        outs[0].append(k.reshape(bsz, seq, N_HEADS_A, DV_A))
        outs[1].append(v.reshape(bsz, seq, N_HEADS_A, DV_A))
        outs[2].append(st.reshape(bsz, N_HEADS_B, HEADDIM_B, D_STATE))
        outs[3].append(b3(xbc)[:, seq - (CONV_W - 1):, :])

        q, k, v, kb, vb, z, xbc, gates, dtr = _proj(hs, row(norm_mix_pre[l]), w_proj)
        k4 = k.reshape(n_smp, n_new, N_HEADS_A, DV_A)
        v4 = v.reshape(n_smp, n_new, N_HEADS_A, DV_A)
        attn_o = _attn_sample(l, q.reshape(n_smp, n_new, ATTN_W), k4, v4, cache_k, cache_v,
                              page_table, rel_bias, lams, subln, lam_init)
        xp = jnp.concatenate([state_conv[l], xbc.reshape(n_smp, n_new, CONV_DIM)], axis=1)
        ssm_o, st = _ssd_sample(l, xp, dtr, z, state_ssm, *ssd_w)
        hs = _post(hs, attn_o, ssm_o, gates, *post_w)
        outs[4].append(k4)
        outs[5].append(v4)
        outs[6].append(st)
        outs[7].append(xp[:, n_new:, :])

    stk = [jnp.stack(o) for o in outs]
    return (hp.reshape(bsz, seq, D_MODEL), hs.reshape(n_smp, n_new, D_MODEL),
            stk[0], stk[1], stk[2], stk[3], stk[4], stk[5], stk[6], stk[7])
```
